```python
import jax, jax.numpy as jnp
from jax import lax
import numpy as np

D_MODEL = 1024
BATCH = 8
SEQ = 4096
DEPTH = 4

PLE_DIM = 256
CHUNK = 128
A_WIDTH = D_MODEL
A_GROUPS = 8
A_GROUP_DIM = A_WIDTH // A_GROUPS
B_HEAD_DIM = 64
B_HEADS = D_MODEL // B_HEAD_DIM
B_WIDTH = B_HEADS * B_HEAD_DIM
Q_BLOCK = 128
IN_SIZES = (A_WIDTH, A_WIDTH, A_WIDTH, B_WIDTH, B_WIDTH, B_WIDTH, B_WIDTH, D_MODEL, D_MODEL)
IN_WIDTH = 3 * A_WIDTH + 4 * B_WIDTH + 2 * D_MODEL
ALPHA = (2 * DEPTH) ** 0.25
BETA = (8 * DEPTH) ** -0.25
LN_EPS = 1e-5

kernel_name = "gated_gmlp_stickbreaking_deepnorm_trunk"


def _layer_norm(x, g, b):
    xf = x.astype(jnp.float32)
    mu = jnp.mean(xf, axis=-1, keepdims=True)
    var = jnp.mean(jnp.square(xf - mu), axis=-1, keepdims=True)
    return ((xf - mu) * lax.rsqrt(var + LN_EPS) * g.astype(jnp.float32) + b.astype(jnp.float32)).astype(x.dtype)


def _chunked_spatial_gating(u, v, vn_g, vn_b, w_s, b_s):
    bsz, seq, _ = v.shape
    vn = _layer_norm(v, vn_g, vn_b).reshape(bsz, seq // CHUNK, CHUNK, A_GROUPS, A_GROUP_DIM)
    causal = jnp.tril(jnp.ones((CHUNK, CHUNK), dtype=bool))
    ws = jnp.where(causal, w_s, jnp.zeros((), w_s.dtype))
    mixed = jnp.einsum('gts,bcsgd->bctgd', ws, vn) + b_s.T[None, None, :, :, None]
    return u * mixed.reshape(bsz, seq, A_WIDTH)


def _stick_breaking_attention(q, k, v):
    bsz, seq, _ = q.shape

    def heads(t):
        return t.reshape(bsz, seq, B_HEADS, B_HEAD_DIM).transpose(0, 2, 1, 3).astype(jnp.float32)

    qh, kh, vh = heads(q), heads(k), heads(v)
    scale = B_HEAD_DIM ** -0.5
    outs = []
    for start in range(0, seq, Q_BLOCK):
        end = start + Q_BLOCK
        z = jnp.einsum('bhtd,bhsd->bhts', qh[:, :, start:end], kh[:, :, :end]) * scale
        t_pos = start + jnp.arange(Q_BLOCK)[:, None]
        s_pos = jnp.arange(end)[None, :]
        strict = s_pos < t_pos
        log_keep = jnp.where(strict, jax.nn.log_sigmoid(-z), 0.0)
        between = lax.cumsum(log_keep, axis=3, reverse=True) - log_keep
        weights = jnp.where(strict, jnp.exp(jax.nn.log_sigmoid(z) + between), 0.0)
        outs.append(jnp.einsum('bhts,bhsd->bhtd', weights, vh[:, :, :end]))
    o = jnp.concatenate(outs, axis=2)
    return o.transpose(0, 2, 1, 3).reshape(bsz, seq, B_WIDTH).astype(v.dtype)


def _layer(x, p_i, w_in, vn_g, vn_b, w_s, b_s, w_pa, w_pb, w_out, w_pe, w_pg, ln_g, ln_b):
    proj = jnp.einsum('bsd,dn->bsn', x, w_in)
    split_points = [int(s) for s in np.cumsum(IN_SIZES)[:-1]]
    u, v, gate_a, q, k, v_b, gate_b, merge_a, merge_b = jnp.split(proj, split_points, axis=-1)
    y_a = _chunked_spatial_gating(u, v, vn_g, vn_b, w_s, b_s) * jax.nn.silu(gate_a)
    y_b = _stick_breaking_attention(q, k, v_b) * jax.nn.silu(gate_b)
    merged = (jax.nn.sigmoid(merge_a) * jnp.einsum('bsc,cd->bsd', y_a, w_pa)
              + jax.nn.sigmoid(merge_b) * jnp.einsum('bsc,cd->bsd', y_b, w_pb))
    y = jnp.einsum('bsd,de->bse', merged, w_out)
    h = ALPHA * x + y
    h = h + jnp.einsum('bsp,pd->bsd', p_i, w_pe) * jax.nn.sigmoid(jnp.einsum('bsd,de->bse', h, w_pg))
    return _layer_norm(h, ln_g, ln_b)


def setup_inputs(seed: int = 0) -> dict:
    key = jax.random.key(seed)
    ks = jax.random.split(key, 15)
    f32 = jnp.float32
    d_sc = D_MODEL ** -0.5
    return {
        "x": jax.random.normal(ks[0], (BATCH, SEQ, D_MODEL), f32),
        "p": jax.random.normal(ks[1], (DEPTH, BATCH, SEQ, PLE_DIM), f32),
        "w_in": jax.random.normal(ks[2], (DEPTH, D_MODEL, IN_WIDTH), f32) * d_sc,
        "vn_g": 1.0 + 0.02 * jax.random.normal(ks[3], (DEPTH, A_WIDTH), f32),
        "vn_b": 0.02 * jax.random.normal(ks[4], (DEPTH, A_WIDTH), f32),
        "w_s": jax.random.normal(ks[5], (DEPTH, A_GROUPS, CHUNK, CHUNK), f32) * CHUNK ** -0.5,
        "b_s": 1.0 + 0.02 * jax.random.normal(ks[6], (DEPTH, A_GROUPS, CHUNK), f32),
        "w_pa": jax.random.normal(ks[7], (DEPTH, A_WIDTH, D_MODEL), f32) * A_WIDTH ** -0.5,
        "w_pb": jax.random.normal(ks[8], (DEPTH, B_WIDTH, D_MODEL), f32) * B_WIDTH ** -0.5,
        "w_out": jax.random.normal(ks[9], (DEPTH, D_MODEL, D_MODEL), f32) * d_sc * BETA,
        "w_pe": jax.random.normal(ks[10], (DEPTH, PLE_DIM, D_MODEL), f32) * PLE_DIM ** -0.5,
        "w_pg": jax.random.normal(ks[11], (DEPTH, D_MODEL, D_MODEL), f32) * d_sc,
        "ln_g": 1.0 + 0.02 * jax.random.normal(ks[12], (DEPTH, D_MODEL), f32),
        "ln_b": 0.02 * jax.random.normal(ks[13], (DEPTH, D_MODEL), f32),
    }


def reference(x, p, w_in, vn_g, vn_b, w_s, b_s, w_pa, w_pb, w_out, w_pe, w_pg, ln_g, ln_b):
    for i in range(DEPTH):
        x = _layer(x, p[i], w_in[i], vn_g[i], vn_b[i], w_s[i], b_s[i], w_pa[i], w_pb[i],
                   w_out[i], w_pe[i], w_pg[i], ln_g[i], ln_b[i])
    return x
```

```python
import functools

import jax
import jax.numpy as jnp
from jax import lax
from jax.experimental import pallas as pl
from jax.experimental.pallas import tpu as pltpu

F32 = jnp.float32
BF16 = jnp.bfloat16

D_MODEL = 1024
DEPTH = 4
PLE_DIM = 256
CHUNK = 128
A_GROUPS = 8
HEAD_DIM = 64
LANES = 128
IN_WIDTH = 9 * D_MODEL
ALPHA = (2 * DEPTH) ** 0.25
LN_EPS = 1e-5

COL_U, COL_V, COL_GA, COL_Q, COL_K, COL_VB, COL_GB, COL_MA, COL_MB = range(9)

PROJ_TM = 1024
PROJ_TN = 1024
ATT_T = 256
TAIL_TM = 256
VMEM_LIMIT = 48 * 1024 * 1024


def _dot(a, b):
    return jnp.dot(a, b, preferred_element_type=F32)


def _dot_nt(a, b):
    return lax.dot_general(a, b, (((1,), (1,)), ((), ())), preferred_element_type=F32)


def _in_proj_kernel(x_ref, w_ref, o_ref):
    o_ref[...] = _dot(x_ref[...].astype(BF16), w_ref[...]).astype(BF16)


def _in_proj(x2d, w_bf16):
    n = x2d.shape[0]
    return pl.pallas_call(
        _in_proj_kernel,
        out_shape=jax.ShapeDtypeStruct((n, IN_WIDTH), BF16),
        grid=(n // PROJ_TM, IN_WIDTH // PROJ_TN),
        in_specs=[
            pl.BlockSpec((PROJ_TM, D_MODEL), lambda i, j: (i, 0)),
            pl.BlockSpec((D_MODEL, PROJ_TN), lambda i, j: (0, j)),
        ],
        out_specs=pl.BlockSpec((PROJ_TM, PROJ_TN), lambda i, j: (i, j)),
        compiler_params=pltpu.CompilerParams(
            dimension_semantics=("arbitrary", "arbitrary"), vmem_limit_bytes=VMEM_LIMIT),
        name="in_proj",
    )(x2d, w_bf16)


def _softplus(s):
    return jnp.maximum(s, 0.0) + jnp.log(1.0 + jnp.exp(-jnp.abs(s)))


def _attn_block(qh, kb, vb, carry, upper, mask):
    s = _dot_nt(qh, kb)
    sp = _softplus(s)
    spm = sp if mask is None else jnp.where(mask, sp, 0.0)
    hi = spm.astype(BF16)
    lo = (spm - hi.astype(F32)).astype(BF16)
    later = _dot(hi, upper) + _dot(lo, upper)
    w = jnp.exp((s - sp) - later + carry)
    if mask is not None:
        w = jnp.where(mask, w, 0.0)
    pv = _dot(w.astype(BF16), vb)
    total = later[:, 0:1] + spm[:, 0:1]
    return pv, carry - total


def _attn_kernel(q_ref, k_ref, v_ref, g_ref, o_ref, acc_ref, carry_ref):
    t = ATT_T
    qi = pl.program_id(2)
    lane = lax.broadcasted_iota(jnp.int32, (t, LANES), 1)
    head0 = lane < HEAD_DIM
    q = q_ref[...] * jnp.asarray(HEAD_DIM ** -0.5, BF16)
    zero = jnp.zeros_like(q)
    qh = (jnp.where(head0, q, zero), jnp.where(head0, zero, q))
    row = lax.broadcasted_iota(jnp.int32, (t, t), 0)
    col = lax.broadcasted_iota(jnp.int32, (t, t), 1)
    upper = (row > col).astype(BF16)
    strict = col < row

    start = pl.multiple_of(qi * t, t)
    kb = k_ref[pl.ds(start, t), :]
    vb = v_ref[pl.ds(start, t), :]
    for h in range(2):
        pv, c = _attn_block(qh[h], kb, vb, jnp.zeros((t, 1), F32), upper, strict)
        acc_ref[h] = pv
        carry_ref[h] = c

    def body(j, _):
        s0 = pl.multiple_of((qi - 1 - j) * t, t)
        kj = k_ref[pl.ds(s0, t), :]
        vj = v_ref[pl.ds(s0, t), :]
        for h in range(2):
            pv, c = _attn_block(qh[h], kj, vj, carry_ref[h], upper, None)
            acc_ref[h] += pv
            carry_ref[h] = c
        return 0

    lax.fori_loop(0, qi, body, 0)

    o = jnp.where(head0, acc_ref[0], acc_ref[1])
    g = g_ref[...].astype(F32)
    o_ref[...] = (o * (g * jax.nn.sigmoid(g))).astype(BF16)


def _attention(proj, batch, seq):
    n = proj.shape[0]
    t = ATT_T
    qt = seq // t
    hg = D_MODEL // LANES
    cpb = D_MODEL // LANES
    return pl.pallas_call(
        _attn_kernel,
        out_shape=jax.ShapeDtypeStruct((n, D_MODEL), BF16),
        grid=(batch, hg, qt),
        in_specs=[
            pl.BlockSpec((t, LANES), lambda b, h, i: (b * qt + i, COL_Q * cpb + h)),
            pl.BlockSpec((seq, LANES), lambda b, h, i: (b, COL_K * cpb + h)),
            pl.BlockSpec((seq, LANES), lambda b, h, i: (b, COL_VB * cpb + h)),
            pl.BlockSpec((t, LANES), lambda b, h, i: (b * qt + i, COL_GB * cpb + h)),
        ],
        out_specs=pl.BlockSpec((t, LANES), lambda b, h, i: (b * qt + i, h)),
        scratch_shapes=[pltpu.VMEM((2, t, LANES), F32), pltpu.VMEM((2, t, 1), F32)],
        compiler_params=pltpu.CompilerParams(
            dimension_semantics=("arbitrary", "arbitrary", "arbitrary"),
            vmem_limit_bytes=VMEM_LIMIT),
        name="attention",
    )(proj, proj, proj, proj)


def _layer_norm(x, g, b):
    mu = jnp.mean(x, axis=-1, keepdims=True)
    d = x - mu
    var = jnp.mean(d * d, axis=-1, keepdims=True)
    return d * lax.rsqrt(var + LN_EPS) * g + b


def _tail_kernel(u_ref, v_ref, ga_ref, ma_ref, mb_ref, yb_ref, x_ref, p_ref,
                 vng_ref, vnb_ref, ws_ref, bst_ref, wpa_ref, wpb_ref, wout_ref, wpg_ref,
                 wpe_ref, lng_ref, lnb_ref, o_ref, vn_ref, ya_ref):
    tm = TAIL_TM
    vn_ref[...] = _layer_norm(v_ref[...].astype(F32), vng_ref[...], vnb_ref[...]).astype(BF16)
    row = lax.broadcasted_iota(jnp.int32, (CHUNK, CHUNK), 0)
    col = lax.broadcasted_iota(jnp.int32, (CHUNK, CHUNK), 1)
    causal = col <= row
    for g in range(A_GROUPS):
        cs = slice(g * LANES, (g + 1) * LANES)
        wsg = jnp.where(causal, ws_ref[g], jnp.zeros((CHUNK, CHUNK), BF16))
        bias = bst_ref[:, g:g + 1]
        for c in range(tm // CHUNK):
            rs = slice(c * CHUNK, (c + 1) * CHUNK)
            mixed = _dot(wsg, vn_ref[rs, cs]) + bias
            ga = ga_ref[rs, cs].astype(F32)
            ya = u_ref[rs, cs].astype(F32) * mixed * (ga * jax.nn.sigmoid(ga))
            ya_ref[rs, cs] = ya.astype(BF16)
    ya_p = _dot(ya_ref[...], wpa_ref[...])
    yb_p = _dot(yb_ref[...], wpb_ref[...])
    merged = (jax.nn.sigmoid(ma_ref[...].astype(F32)) * ya_p
              + jax.nn.sigmoid(mb_ref[...].astype(F32)) * yb_p)
    y = _dot(merged.astype(BF16), wout_ref[...])
    h = ALPHA * x_ref[...] + y
    gate = jax.nn.sigmoid(_dot(h.astype(BF16), wpg_ref[...]))
    pe = _dot(p_ref[...].astype(BF16), wpe_ref[...])
    h = h + pe * gate
    o_ref[...] = _layer_norm(h, lng_ref[...], lnb_ref[...])


def _tail(proj, yb, x2d, p2d, vn_g, vn_b, ws, bst, w_pa, w_pb, w_out, w_pg, w_pe, ln_g, ln_b):
    n = x2d.shape[0]
    tm = TAIL_TM

    def seg(c):
        return pl.BlockSpec((tm, D_MODEL), lambda i, c=c: (i, c))

    def const(shape):
        return pl.BlockSpec(shape, lambda i, nd=len(shape): (0,) * nd)

    return pl.pallas_call(
        _tail_kernel,
        out_shape=jax.ShapeDtypeStruct((n, D_MODEL), F32),
        grid=(n // tm,),
        in_specs=[
            seg(COL_U), seg(COL_V), seg(COL_GA), seg(COL_MA), seg(COL_MB),
            seg(0), seg(0),
            pl.BlockSpec((tm, PLE_DIM), lambda i: (i, 0)),
            const((1, D_MODEL)), const((1, D_MODEL)),
            const((A_GROUPS, CHUNK, CHUNK)), const((CHUNK, A_GROUPS)),
            const((D_MODEL, D_MODEL)), const((D_MODEL, D_MODEL)),
            const((D_MODEL, D_MODEL)), const((D_MODEL, D_MODEL)),
            const((PLE_DIM, D_MODEL)),
            const((1, D_MODEL)), const((1, D_MODEL)),
        ],
        out_specs=pl.BlockSpec((tm, D_MODEL), lambda i: (i, 0)),
        scratch_shapes=[pltpu.VMEM((tm, D_MODEL), BF16), pltpu.VMEM((tm, D_MODEL), BF16)],
        compiler_params=pltpu.CompilerParams(
            dimension_semantics=("arbitrary",), vmem_limit_bytes=VMEM_LIMIT),
        name="tail",
    )(proj, proj, proj, proj, proj, yb, x2d, p2d, vn_g, vn_b, ws, bst,
      w_pa, w_pb, w_out, w_pg, w_pe, ln_g, ln_b)


def kernel(x, p, w_in, vn_g, vn_b, w_s, b_s, w_pa, w_pb, w_out, w_pe, w_pg, ln_g, ln_b):
    batch, seq, d = x.shape
    n = batch * seq
    assert d == D_MODEL and seq % ATT_T == 0 and n % PROJ_TM == 0 and n % TAIL_TM == 0
    xf = x.reshape(n, d)
    for i in range(DEPTH):
        proj = _in_proj(xf, w_in[i].astype(BF16))
        yb = _attention(proj, batch, seq)
        xf = _tail(
            proj, yb, xf, p[i].reshape(n, PLE_DIM),
            vn_g[i].reshape(1, d), vn_b[i].reshape(1, d),
            w_s[i].astype(BF16), b_s[i].T,
            w_pa[i].astype(BF16), w_pb[i].astype(BF16), w_out[i].astype(BF16),
            w_pg[i].astype(BF16), w_pe[i].astype(BF16),
            ln_g[i].reshape(1, d), ln_b[i].reshape(1, d))
    return xf.reshape(batch, seq, d)
```

```python
import math

import jax
import jax.numpy as jnp
from jax import lax
from jax.experimental import pallas as pl
from jax.experimental.pallas import tpu as pltpu

F32 = jnp.float32
BF16 = jnp.bfloat16

D_MODEL = 1024
DEPTH = 4
PLE_DIM = 256
CHUNK = 128
A_GROUPS = 8
HEAD_DIM = 64
LANES = 128
IN_WIDTH = 9 * D_MODEL
ALPHA = (2 * DEPTH) ** 0.25
LN_EPS = 1e-5
LOG2E = math.log2(math.e)
Q_SCALE = HEAD_DIM ** -0.5 * LOG2E

COL_U, COL_V, COL_GA, COL_Q, COL_K, COL_VB, COL_GB, COL_MA, COL_MB = range(9)

PROJ_TM = 1024
PROJ_TN = 1024
ATT_T = 256
ATT_LG = 8
ATT_SKIP_LOG2 = -150.0
TAIL_TM = 256
VMEM_LIMIT = 48 * 1024 * 1024


def _dot(a, b):
    return jnp.dot(a, b, preferred_element_type=F32)


def _dot_nt(a, b):
    return lax.dot_general(a, b, (((1,), (1,)), ((), ())), preferred_element_type=F32)


def _in_proj_kernel(x_ref, w_ref, o_ref):
    acc = _dot(x_ref[...].astype(BF16), w_ref[...])
    scale = jnp.where(pl.program_id(1) == COL_Q, Q_SCALE, 1.0)
    o_ref[...] = (acc * scale).astype(BF16)


def _in_proj(x2d, w_bf16):
    n = x2d.shape[0]
    return pl.pallas_call(
        _in_proj_kernel,
        out_shape=jax.ShapeDtypeStruct((n, IN_WIDTH), BF16),
        grid=(n // PROJ_TM, IN_WIDTH // PROJ_TN),
        in_specs=[
            pl.BlockSpec((PROJ_TM, D_MODEL), lambda i, j: (i, 0)),
            pl.BlockSpec((D_MODEL, PROJ_TN), lambda i, j: (0, j)),
        ],
        out_specs=pl.BlockSpec((PROJ_TM, PROJ_TN), lambda i, j: (i, j)),
        compiler_params=pltpu.CompilerParams(
            dimension_semantics=("arbitrary", "arbitrary"), vmem_limit_bytes=VMEM_LIMIT),
        name="in_proj",
    )(x2d, w_bf16)


def _attn_block(q2, kb, vb, carry, upper, mask):
    s = _dot_nt(q2, kb)
    sp = jnp.maximum(s, jnp.log(1.0 + jnp.exp2(jnp.minimum(s, 64.0))) * LOG2E)
    spm = sp if mask is None else jnp.where(mask, sp, 0.0)
    later = _dot(spm.astype(BF16), upper)
    w = jnp.exp2((s - sp) + (carry - later))
    if mask is not None:
        w = jnp.where(mask, w, 0.0)
    pv = _dot(w.astype(BF16), vb)
    return pv, carry - (later[:, 0:1] + spm[:, 0:1])


def _attn_kernel(q_ref, k_ref, v_ref, g_ref, o_ref, acc_ref, carry_ref):
    t = ATT_T
    qi = pl.program_id(2)
    lane = lax.broadcasted_iota(jnp.int32, (t, LANES), 1)
    head0 = lane < HEAD_DIM
    row = lax.broadcasted_iota(jnp.int32, (t, t), 0)
    col = lax.broadcasted_iota(jnp.int32, (t, t), 1)
    upper = (row > col).astype(BF16)
    strict1 = col < row
    strict = jnp.concatenate([strict1, strict1], axis=0)

    def lanes(g):
        return slice(g * LANES, (g + 1) * LANES)

    q2 = []
    for g in range(ATT_LG):
        q = q_ref[:, lanes(g)]
        zero = jnp.zeros_like(q)
        q2.append(jnp.concatenate([jnp.where(head0, q, zero), jnp.where(head0, zero, q)], axis=0))

    def tile(ref, idx, g):
        return ref[pl.ds(pl.multiple_of(idx * t, t), t), lanes(g)]

    prev = jnp.maximum(qi - 1, 0)
    has_prev = qi > 0
    m = None
    for g in range(ATT_LG):
        pv0, c0 = _attn_block(q2[g], tile(k_ref, qi, g), tile(v_ref, qi, g),
                              jnp.zeros((2 * t, 1), F32), upper, strict)
        c0 = jnp.where(has_prev, c0, -jnp.inf)
        pv1, c1 = _attn_block(q2[g], tile(k_ref, prev, g), tile(v_ref, prev, g), c0, upper, None)
        acc_ref[g] = pv0 + pv1
        carry_ref[g] = c1
        cm = jnp.max(c1)
        m = cm if m is None else jnp.maximum(m, cm)

    def cond(state):
        idx, worst = state
        return jnp.logical_and(idx >= 0, worst > ATT_SKIP_LOG2)

    def body(state):
        idx, _ = state
        worst = None
        for g in range(ATT_LG):
            pv, c = _attn_block(q2[g], tile(k_ref, idx, g), tile(v_ref, idx, g),
                                carry_ref[g], upper, None)
            acc_ref[g] += pv
            carry_ref[g] = c
            cm = jnp.max(c)
            worst = cm if worst is None else jnp.maximum(worst, cm)
        return idx - 1, worst

    lax.while_loop(cond, body, (qi - 2, m))

    for g in range(ATT_LG):
        o = jnp.where(head0, acc_ref[g, 0:t, :], acc_ref[g, t:2 * t, :])
        gate = g_ref[:, lanes(g)].astype(F32)
        o_ref[:, lanes(g)] = (o * (gate * jax.nn.sigmoid(gate))).astype(BF16)


def _attention(proj, batch, seq):
    n = proj.shape[0]
    t = ATT_T
    qt = seq // t
    w = ATT_LG * LANES
    hg = D_MODEL // w
    cpb = D_MODEL // w
    return pl.pallas_call(
        _attn_kernel,
        out_shape=jax.ShapeDtypeStruct((n, D_MODEL), BF16),
        grid=(batch, hg, qt),
        in_specs=[
            pl.BlockSpec((t, w), lambda b, h, i: (b * qt + i, COL_Q * cpb + h)),
            pl.BlockSpec((seq, w), lambda b, h, i: (b, COL_K * cpb + h)),
            pl.BlockSpec((seq, w), lambda b, h, i: (b, COL_VB * cpb + h)),
            pl.BlockSpec((t, w), lambda b, h, i: (b * qt + i, COL_GB * cpb + h)),
        ],
        out_specs=pl.BlockSpec((t, w), lambda b, h, i: (b * qt + i, h)),
        scratch_shapes=[pltpu.VMEM((ATT_LG, 2 * t, LANES), F32),
                        pltpu.VMEM((ATT_LG, 2 * t, 1), F32)],
        compiler_params=pltpu.CompilerParams(
            dimension_semantics=("arbitrary", "arbitrary", "arbitrary"),
            vmem_limit_bytes=VMEM_LIMIT),
        name="attention",
    )(proj, proj, proj, proj)


def _layer_norm(x, g, b):
    mu = jnp.mean(x, axis=-1, keepdims=True)
    d = x - mu
    var = jnp.mean(d * d, axis=-1, keepdims=True)
    return d * lax.rsqrt(var + LN_EPS) * g + b


def _tail_kernel(u_ref, v_ref, ga_ref, ma_ref, mb_ref, yb_ref, x_ref, p_ref,
                 vng_ref, vnb_ref, ws_ref, bst_ref, wpa_ref, wpb_ref, wout_ref, wpg_ref,
                 wpe_ref, lng_ref, lnb_ref, o_ref, vn_ref, ya_ref):
    tm = TAIL_TM
    vn_ref[...] = _layer_norm(v_ref[...].astype(F32), vng_ref[...], vnb_ref[...]).astype(BF16)
    row = lax.broadcasted_iota(jnp.int32, (CHUNK, CHUNK), 0)
    col = lax.broadcasted_iota(jnp.int32, (CHUNK, CHUNK), 1)
    causal = col <= row
    for g in range(A_GROUPS):
        cs = slice(g * LANES, (g + 1) * LANES)
        wsg = jnp.where(causal, ws_ref[g], jnp.zeros((CHUNK, CHUNK), BF16))
        bias = bst_ref[:, g:g + 1]
        for c in range(tm // CHUNK):
            rs = slice(c * CHUNK, (c + 1) * CHUNK)
            mixed = _dot(wsg, vn_ref[rs, cs]) + bias
            ga = ga_ref[rs, cs].astype(F32)
            ya = u_ref[rs, cs].astype(F32) * mixed * (ga * jax.nn.sigmoid(ga))
            ya_ref[rs, cs] = ya.astype(BF16)
    ya_p = _dot(ya_ref[...], wpa_ref[...])
    yb_p = _dot(yb_ref[...], wpb_ref[...])
    merged = (jax.nn.sigmoid(ma_ref[...].astype(F32)) * ya_p
              + jax.nn.sigmoid(mb_ref[...].astype(F32)) * yb_p)
    y = _dot(merged.astype(BF16), wout_ref[...])
    h = ALPHA * x_ref[...] + y
    gate = jax.nn.sigmoid(_dot(h.astype(BF16), wpg_ref[...]))
    pe = _dot(p_ref[...].astype(BF16), wpe_ref[...])
    h = h + pe * gate
    o_ref[...] = _layer_norm(h, lng_ref[...], lnb_ref[...])


def _tail(proj, yb, x2d, p2d, vn_g, vn_b, ws, bst, w_pa, w_pb, w_out, w_pg, w_pe, ln_g, ln_b):
    n = x2d.shape[0]
    tm = TAIL_TM

    def seg(c):
        return pl.BlockSpec((tm, D_MODEL), lambda i, c=c: (i, c))

    def const(shape):
        return pl.BlockSpec(shape, lambda i, nd=len(shape): (0,) * nd)

    return pl.pallas_call(
        _tail_kernel,
        out_shape=jax.ShapeDtypeStruct((n, D_MODEL), F32),
        grid=(n // tm,),
        in_specs=[
            seg(COL_U), seg(COL_V), seg(COL_GA), seg(COL_MA), seg(COL_MB),
            seg(0), seg(0),
            pl.BlockSpec((tm, PLE_DIM), lambda i: (i, 0)),
            const((1, D_MODEL)), const((1, D_MODEL)),
            const((A_GROUPS, CHUNK, CHUNK)), const((CHUNK, A_GROUPS)),
            const((D_MODEL, D_MODEL)), const((D_MODEL, D_MODEL)),
            const((D_MODEL, D_MODEL)), const((D_MODEL, D_MODEL)),
            const((PLE_DIM, D_MODEL)),
            const((1, D_MODEL)), const((1, D_MODEL)),
        ],
        out_specs=pl.BlockSpec((tm, D_MODEL), lambda i: (i, 0)),
        scratch_shapes=[pltpu.VMEM((tm, D_MODEL), BF16), pltpu.VMEM((tm, D_MODEL), BF16)],
        compiler_params=pltpu.CompilerParams(
            dimension_semantics=("arbitrary",), vmem_limit_bytes=VMEM_LIMIT),
        name="tail",
    )(proj, proj, proj, proj, proj, yb, x2d, p2d, vn_g, vn_b, ws, bst,
      w_pa, w_pb, w_out, w_pg, w_pe, ln_g, ln_b)


def kernel(x, p, w_in, vn_g, vn_b, w_s, b_s, w_pa, w_pb, w_out, w_pe, w_pg, ln_g, ln_b):
    batch, seq, d = x.shape
    n = batch * seq
    assert d == D_MODEL and seq % ATT_T == 0 and n % PROJ_TM == 0 and n % TAIL_TM == 0
    assert PROJ_TN == D_MODEL
    xf = x.reshape(n, d)
    for i in range(DEPTH):
        proj = _in_proj(xf, w_in[i].astype(BF16))
        yb = _attention(proj, batch, seq)
        xf = _tail(
            proj, yb, xf, p[i].reshape(n, PLE_DIM),
            vn_g[i].reshape(1, d), vn_b[i].reshape(1, d),
            w_s[i].astype(BF16), b_s[i].T,
            w_pa[i].astype(BF16), w_pb[i].astype(BF16), w_out[i].astype(BF16),
            w_pg[i].astype(BF16), w_pe[i].astype(BF16),
            ln_g[i].reshape(1, d), ln_b[i].reshape(1, d))
    return xf.reshape(batch, seq, d)
```

```python
import math

import jax
import jax.numpy as jnp
from jax import lax
from jax.experimental import pallas as pl
from jax.experimental.pallas import tpu as pltpu

F32 = jnp.float32
BF16 = jnp.bfloat16

D_MODEL = 1024
DEPTH = 4
PLE_DIM = 256
CHUNK = 128
A_GROUPS = 8
HEAD_DIM = 64
LANES = 128
IN_WIDTH = 9 * D_MODEL
ALPHA = (2 * DEPTH) ** 0.25
LN_EPS = 1e-5
LOG2E = math.log2(math.e)
Q_SCALE = HEAD_DIM ** -0.5 * LOG2E

COL_U, COL_V, COL_GA, COL_Q, COL_K, COL_VB, COL_GB, COL_MA, COL_MB = range(9)

PROJ_TM = 2048
PROJ_TN = 1024
ATT_T = 256
ATT_LG = 8
ATT_SKIP_LOG2 = -150.0
TAIL_TM = 512
TAIL_SUB = 256
VMEM_LIMIT = 48 * 1024 * 1024


def _dot(a, b):
    return jnp.dot(a, b, preferred_element_type=F32)


def _dot_nt(a, b):
    return lax.dot_general(a, b, (((1,), (1,)), ((), ())), preferred_element_type=F32)


def _layer_spec(layer, shape):
    return pl.BlockSpec((None,) + shape, lambda *_: (layer,) + (0,) * len(shape))


def _in_proj_kernel(x_ref, w_ref, o_ref):
    acc = _dot(x_ref[...].astype(BF16), w_ref[...])
    scale = jnp.where(pl.program_id(1) == COL_Q, Q_SCALE, 1.0)
    o_ref[...] = (acc * scale).astype(BF16)


def _in_proj(x2d, w_in, layer):
    n = x2d.shape[0]
    return pl.pallas_call(
        _in_proj_kernel,
        out_shape=jax.ShapeDtypeStruct((n, IN_WIDTH), BF16),
        grid=(n // PROJ_TM, IN_WIDTH // PROJ_TN),
        in_specs=[
            pl.BlockSpec((PROJ_TM, D_MODEL), lambda i, j: (i, 0)),
            pl.BlockSpec((None, D_MODEL, PROJ_TN), lambda i, j: (layer, 0, j)),
        ],
        out_specs=pl.BlockSpec((PROJ_TM, PROJ_TN), lambda i, j: (i, j)),
        compiler_params=pltpu.CompilerParams(
            dimension_semantics=("arbitrary", "arbitrary"), vmem_limit_bytes=VMEM_LIMIT),
        name="in_proj",
    )(x2d, w_in)


def _attn_block(q2, kb, vb, carry, upper, mask):
    s = _dot_nt(q2, kb)
    sp = jnp.maximum(s, jnp.log(1.0 + jnp.exp2(jnp.minimum(s, 64.0))) * LOG2E)
    spm = sp if mask is None else jnp.where(mask, sp, 0.0)
    later = _dot(spm.astype(BF16), upper)
    w = jnp.exp2((s - sp) + (carry - later))
    if mask is not None:
        w = jnp.where(mask, w, 0.0)
    pv = _dot(w.astype(BF16), vb)
    return pv, carry - (later[:, 0:1] + spm[:, 0:1])


def _attn_kernel(q_ref, k_ref, v_ref, g_ref, o_ref, acc_ref, carry_ref):
    t = ATT_T
    qi = pl.program_id(2)
    lane = lax.broadcasted_iota(jnp.int32, (t, LANES), 1)
    head0 = lane < HEAD_DIM
    row = lax.broadcasted_iota(jnp.int32, (t, t), 0)
    col = lax.broadcasted_iota(jnp.int32, (t, t), 1)
    upper = (row > col).astype(BF16)
    strict1 = col < row
    strict = jnp.concatenate([strict1, strict1], axis=0)

    def lanes(g):
        return slice(g * LANES, (g + 1) * LANES)

    q2 = []
    for g in range(ATT_LG):
        q = q_ref[:, lanes(g)]
        zero = jnp.zeros_like(q)
        q2.append(jnp.concatenate([jnp.where(head0, q, zero), jnp.where(head0, zero, q)], axis=0))

    def tile(ref, idx, g):
        return ref[pl.ds(pl.multiple_of(idx * t, t), t), lanes(g)]

    prev = jnp.maximum(qi - 1, 0)
    has_prev = qi > 0
    m = None
    for g in range(ATT_LG):
        pv0, c0 = _attn_block(q2[g], tile(k_ref, qi, g), tile(v_ref, qi, g),
                              jnp.zeros((2 * t, 1), F32), upper, strict)
        c0 = jnp.where(has_prev, c0, -jnp.inf)
        pv1, c1 = _attn_block(q2[g], tile(k_ref, prev, g), tile(v_ref, prev, g), c0, upper, None)
        acc_ref[g] = pv0 + pv1
        carry_ref[g] = c1
        cm = jnp.max(c1)
        m = cm if m is None else jnp.maximum(m, cm)

    def cond(state):
        idx, worst = state
        return jnp.logical_and(idx >= 0, worst > ATT_SKIP_LOG2)

    def body(state):
        idx, _ = state
        worst = None
        for g in range(ATT_LG):
            pv, c = _attn_block(q2[g], tile(k_ref, idx, g), tile(v_ref, idx, g),
                                carry_ref[g], upper, None)
            acc_ref[g] += pv
            carry_ref[g] = c
            cm = jnp.max(c)
            worst = cm if worst is None else jnp.maximum(worst, cm)
        return idx - 1, worst

    lax.while_loop(cond, body, (qi - 2, m))

    for g in range(ATT_LG):
        o = jnp.where(head0, acc_ref[g, 0:t, :], acc_ref[g, t:2 * t, :])
        gate = g_ref[:, lanes(g)].astype(F32)
        o_ref[:, lanes(g)] = (o * (gate * jax.nn.sigmoid(gate))).astype(BF16)


def _attention(proj, batch, seq):
    n = proj.shape[0]
    t = ATT_T
    qt = seq // t
    w = ATT_LG * LANES
    hg = D_MODEL // w
    cpb = D_MODEL // w
    return pl.pallas_call(
        _attn_kernel,
        out_shape=jax.ShapeDtypeStruct((n, D_MODEL), BF16),
        grid=(batch, hg, qt),
        in_specs=[
            pl.BlockSpec((t, w), lambda b, h, i: (b * qt + i, COL_Q * cpb + h)),
            pl.BlockSpec((seq, w), lambda b, h, i: (b, COL_K * cpb + h)),
            pl.BlockSpec((seq, w), lambda b, h, i: (b, COL_VB * cpb + h)),
            pl.BlockSpec((t, w), lambda b, h, i: (b * qt + i, COL_GB * cpb + h)),
        ],
        out_specs=pl.BlockSpec((t, w), lambda b, h, i: (b * qt + i, h)),
        scratch_shapes=[pltpu.VMEM((ATT_LG, 2 * t, LANES), F32),
                        pltpu.VMEM((ATT_LG, 2 * t, 1), F32)],
        compiler_params=pltpu.CompilerParams(
            dimension_semantics=("arbitrary", "arbitrary", "arbitrary"),
            vmem_limit_bytes=VMEM_LIMIT),
        name="attention",
    )(proj, proj, proj, proj)


def _layer_norm(x, g, b):
    mu = jnp.mean(x, axis=-1, keepdims=True)
    d = x - mu
    var = jnp.mean(d * d, axis=-1, keepdims=True)
    return d * lax.rsqrt(var + LN_EPS) * g + b


def _tail_kernel(u_ref, v_ref, ga_ref, ma_ref, mb_ref, yb_ref, x_ref, p_ref,
                 vng_ref, vnb_ref, ws_ref, bst_ref, wpa_ref, wpb_ref, wout_ref, wpg_ref,
                 wpe_ref, lng_ref, lnb_ref, o_ref, vn_ref, ya_ref):
    row = lax.broadcasted_iota(jnp.int32, (CHUNK, CHUNK), 0)
    col = lax.broadcasted_iota(jnp.int32, (CHUNK, CHUNK), 1)
    causal = col <= row
    for r0 in range(0, TAIL_TM, TAIL_SUB):
        rows = slice(r0, r0 + TAIL_SUB)
        vn_ref[rows, :] = _layer_norm(
            v_ref[rows, :].astype(F32), vng_ref[...], vnb_ref[...]).astype(BF16)
        for g in range(A_GROUPS):
            cs = slice(g * LANES, (g + 1) * LANES)
            wsg = jnp.where(causal, ws_ref[g], jnp.zeros((CHUNK, CHUNK), BF16))
            bias = bst_ref[:, g:g + 1]
            for c0 in range(r0, r0 + TAIL_SUB, CHUNK):
                rs = slice(c0, c0 + CHUNK)
                mixed = _dot(wsg, vn_ref[rs, cs]) + bias
                ga = ga_ref[rs, cs].astype(F32)
                ya = u_ref[rs, cs].astype(F32) * mixed * (ga * jax.nn.sigmoid(ga))
                ya_ref[rs, cs] = ya.astype(BF16)
        ya_p = _dot(ya_ref[rows, :], wpa_ref[...])
        yb_p = _dot(yb_ref[rows, :], wpb_ref[...])
        merged = (jax.nn.sigmoid(ma_ref[rows, :].astype(F32)) * ya_p
                  + jax.nn.sigmoid(mb_ref[rows, :].astype(F32)) * yb_p)
        y = _dot(merged.astype(BF16), wout_ref[...])
        h = ALPHA * x_ref[rows, :] + y
        gate = jax.nn.sigmoid(_dot(h.astype(BF16), wpg_ref[...]))
        pe = _dot(p_ref[rows, :].astype(BF16), wpe_ref[...])
        h = h + pe * gate
        o_ref[rows, :] = _layer_norm(h, lng_ref[...], lnb_ref[...])


def _tail(proj, yb, x2d, p3d, layer, vn_g, vn_b, ws, bst, w_pa, w_pb, w_out, w_pg, w_pe,
          ln_g, ln_b):
    n = x2d.shape[0]
    tm = TAIL_TM

    def seg(c):
        return pl.BlockSpec((tm, D_MODEL), lambda i, c=c: (i, c))

    def const(shape):
        return _layer_spec(layer, shape)

    return pl.pallas_call(
        _tail_kernel,
        out_shape=jax.ShapeDtypeStruct((n, D_MODEL), F32),
        grid=(n // tm,),
        in_specs=[
            seg(COL_U), seg(COL_V), seg(COL_GA), seg(COL_MA), seg(COL_MB),
            seg(0), seg(0),
            pl.BlockSpec((None, tm, PLE_DIM), lambda i: (layer, i, 0)),
            const((1, D_MODEL)), const((1, D_MODEL)),
            const((A_GROUPS, CHUNK, CHUNK)), const((CHUNK, A_GROUPS)),
            const((D_MODEL, D_MODEL)), const((D_MODEL, D_MODEL)),
            const((D_MODEL, D_MODEL)), const((D_MODEL, D_MODEL)),
            const((PLE_DIM, D_MODEL)),
            const((1, D_MODEL)), const((1, D_MODEL)),
        ],
        out_specs=pl.BlockSpec((tm, D_MODEL), lambda i: (i, 0)),
        scratch_shapes=[pltpu.VMEM((tm, D_MODEL), BF16), pltpu.VMEM((tm, D_MODEL), BF16)],
        compiler_params=pltpu.CompilerParams(
            dimension_semantics=("arbitrary",), vmem_limit_bytes=VMEM_LIMIT),
        name="tail",
    )(proj, proj, proj, proj, proj, yb, x2d, p3d, vn_g, vn_b, ws, bst,
      w_pa, w_pb, w_out, w_pg, w_pe, ln_g, ln_b)


def kernel(x, p, w_in, vn_g, vn_b, w_s, b_s, w_pa, w_pb, w_out, w_pe, w_pg, ln_g, ln_b):
    batch, seq, d = x.shape
    n = batch * seq
    assert d == D_MODEL and seq % ATT_T == 0 and n % PROJ_TM == 0 and n % TAIL_TM == 0
    assert PROJ_TN == D_MODEL
    depth = w_in.shape[0]
    xf = x.reshape(n, d)
    p3d = p.reshape(depth, n, PLE_DIM)
    row = lambda a: a.reshape(depth, 1, d)
    w_in, w_s, w_pa, w_pb, w_out, w_pg, w_pe = (
        a.astype(BF16) for a in (w_in, w_s, w_pa, w_pb, w_out, w_pg, w_pe))
    vn_g, vn_b, ln_g, ln_b = row(vn_g), row(vn_b), row(ln_g), row(ln_b)
    bst = jnp.swapaxes(b_s, 1, 2)
    for i in range(depth):
        proj = _in_proj(xf, w_in, i)
        yb = _attention(proj, batch, seq)
        xf = _tail(proj, yb, xf, p3d, i, vn_g, vn_b, w_s, bst, w_pa, w_pb, w_out, w_pg, w_pe,
                   ln_g, ln_b)
    return xf.reshape(batch, seq, d)
```

```python
import math

import jax
import jax.numpy as jnp
from jax import lax
from jax.experimental import pallas as pl
from jax.experimental.pallas import tpu as pltpu

F32 = jnp.float32
BF16 = jnp.bfloat16

D_MODEL = 1024
DEPTH = 4
PLE_DIM = 256
CHUNK = 128
A_GROUPS = 8
HEAD_DIM = 64
LANES = 128
IN_WIDTH = 9 * D_MODEL
ALPHA = (2 * DEPTH) ** 0.25
LN_EPS = 1e-5
LOG2E = math.log2(math.e)
Q_SCALE = HEAD_DIM ** -0.5 * LOG2E

COL_U, COL_V, COL_GA, COL_Q, COL_K, COL_VB, COL_GB, COL_MA, COL_MB = range(9)

PROJ_TM = 2048
PROJ_TN = 1024
ATT_T = 256
ATT_LG = 8
ATT_QT = 2
ATT_SKIP_LOG2 = -150.0
TAIL_TM = 512
TAIL_SUB = 256
VMEM_LIMIT = 48 * 1024 * 1024
ATT_VMEM_LIMIT = 60 * 1024 * 1024


def _dot(a, b):
    return jnp.dot(a, b, preferred_element_type=F32)


def _dot_nt(a, b):
    return lax.dot_general(a, b, (((1,), (1,)), ((), ())), preferred_element_type=F32)


def _layer_spec(layer, shape):
    return pl.BlockSpec((None,) + shape, lambda *_: (layer,) + (0,) * len(shape))


def _in_proj_kernel(x_ref, w_ref, o_ref):
    acc = _dot(x_ref[...].astype(BF16), w_ref[...].astype(BF16))
    scale = jnp.where(pl.program_id(1) == COL_Q, Q_SCALE, 1.0)
    o_ref[...] = (acc * scale).astype(BF16)


def _in_proj(x2d, w_in, layer):
    n = x2d.shape[0]
    return pl.pallas_call(
        _in_proj_kernel,
        out_shape=jax.ShapeDtypeStruct((n, IN_WIDTH), BF16),
        grid=(n // PROJ_TM, IN_WIDTH // PROJ_TN),
        in_specs=[
            pl.BlockSpec((PROJ_TM, D_MODEL), lambda i, j: (i, 0)),
            pl.BlockSpec((None, D_MODEL, PROJ_TN), lambda i, j: (layer, 0, j)),
        ],
        out_specs=pl.BlockSpec((PROJ_TM, PROJ_TN), lambda i, j: (i, j)),
        compiler_params=pltpu.CompilerParams(
            dimension_semantics=("arbitrary", "arbitrary"), vmem_limit_bytes=VMEM_LIMIT),
        name="in_proj",
    )(x2d, w_in)


def _attn_block(q2, kb, vb, carry, upper, mask):
    s = _dot_nt(q2, kb)
    sp = jnp.maximum(s, jnp.log(1.0 + jnp.exp2(jnp.minimum(s, 64.0))) * LOG2E)
    spm = sp if mask is None else jnp.where(mask, sp, 0.0)
    later = _dot(spm.astype(BF16), upper)
    w = jnp.exp2((s - sp) + (carry - later))
    if mask is not None:
        w = jnp.where(mask, w, 0.0)
    pv = _dot(w.astype(BF16), vb)
    return pv, carry - (later[:, 0:1] + spm[:, 0:1])


def _attn_kernel(q_ref, k_ref, v_ref, g_ref, o_ref, acc_ref, carry_ref):
    t = ATT_T
    step = pl.program_id(2)
    lane = lax.broadcasted_iota(jnp.int32, (t, LANES), 1)
    head0 = lane < HEAD_DIM
    row = lax.broadcasted_iota(jnp.int32, (t, t), 0)
    col = lax.broadcasted_iota(jnp.int32, (t, t), 1)
    upper = (row > col).astype(BF16)
    strict1 = col < row
    strict = jnp.concatenate([strict1, strict1], axis=0)

    def lanes(g):
        return slice(g * LANES, (g + 1) * LANES)

    def tile(ref, idx, g):
        return ref[pl.ds(pl.multiple_of(idx * t, t), t), lanes(g)]

    q2 = {}
    m = None
    for u in range(ATT_QT):
        qi = step * ATT_QT + u
        rows = slice(u * t, (u + 1) * t)
        prev = jnp.maximum(qi - 1, 0)
        has_prev = qi > 0
        for g in range(ATT_LG):
            q = q_ref[rows, lanes(g)]
            zero = jnp.zeros_like(q)
            q2[u, g] = jnp.concatenate(
                [jnp.where(head0, q, zero), jnp.where(head0, zero, q)], axis=0)
            pv0, c0 = _attn_block(q2[u, g], tile(k_ref, qi, g), tile(v_ref, qi, g),
                                  jnp.zeros((2 * t, 1), F32), upper, strict)
            c0 = jnp.where(has_prev, c0, -jnp.inf)
            pv1, c1 = _attn_block(q2[u, g], tile(k_ref, prev, g), tile(v_ref, prev, g),
                                  c0, upper, None)
            acc_ref[u, g] = pv0 + pv1
            carry_ref[u, g] = c1
            cm = jnp.max(c1)
            m = cm if m is None else jnp.maximum(m, cm)

    def cond(state):
        j, worst = state
        return jnp.logical_and((step * ATT_QT + ATT_QT - 1) - 2 - j >= 0, worst > ATT_SKIP_LOG2)

    def body(state):
        j, _ = state
        worst = None
        for u in range(ATT_QT):
            idx = step * ATT_QT + u - 2 - j
            live = jnp.where(idx >= 0, 0.0, -jnp.inf)
            idx = jnp.maximum(idx, 0)
            for g in range(ATT_LG):
                pv, c = _attn_block(q2[u, g], tile(k_ref, idx, g), tile(v_ref, idx, g),
                                    carry_ref[u, g] + live, upper, None)
                acc_ref[u, g] += pv
                carry_ref[u, g] = c
                cm = jnp.max(c)
                worst = cm if worst is None else jnp.maximum(worst, cm)
        return j + 1, worst

    lax.while_loop(cond, body, (0, m))

    for u in range(ATT_QT):
        rows = slice(u * t, (u + 1) * t)
        for g in range(ATT_LG):
            o = jnp.where(head0, acc_ref[u, g, 0:t, :], acc_ref[u, g, t:2 * t, :])
            gate = g_ref[rows, lanes(g)].astype(F32)
            o_ref[rows, lanes(g)] = (o * (gate * jax.nn.sigmoid(gate))).astype(BF16)


def _attention(proj, batch, seq):
    n = proj.shape[0]
    t = ATT_T * ATT_QT
    qt = seq // t
    w = ATT_LG * LANES
    hg = D_MODEL // w
    cpb = D_MODEL // w
    return pl.pallas_call(
        _attn_kernel,
        out_shape=jax.ShapeDtypeStruct((n, D_MODEL), BF16),
        grid=(batch, hg, qt),
        in_specs=[
            pl.BlockSpec((t, w), lambda b, h, i: (b * qt + i, COL_Q * cpb + h)),
            pl.BlockSpec((seq, w), lambda b, h, i: (b, COL_K * cpb + h)),
            pl.BlockSpec((seq, w), lambda b, h, i: (b, COL_VB * cpb + h)),
            pl.BlockSpec((t, w), lambda b, h, i: (b * qt + i, COL_GB * cpb + h)),
        ],
        out_specs=pl.BlockSpec((t, w), lambda b, h, i: (b * qt + i, h)),
        scratch_shapes=[pltpu.VMEM((ATT_QT, ATT_LG, 2 * ATT_T, LANES), F32),
                        pltpu.VMEM((ATT_QT, ATT_LG, 2 * ATT_T, 1), F32)],
        compiler_params=pltpu.CompilerParams(
            dimension_semantics=("arbitrary", "arbitrary", "arbitrary"),
            vmem_limit_bytes=ATT_VMEM_LIMIT),
        name="attention",
    )(proj, proj, proj, proj)


def _layer_norm(x, g, b):
    mu = jnp.mean(x, axis=-1, keepdims=True)
    d = x - mu
    var = jnp.mean(d * d, axis=-1, keepdims=True)
    return d * lax.rsqrt(var + LN_EPS) * g + b


def _tail_kernel(u_ref, v_ref, ga_ref, ma_ref, mb_ref, yb_ref, x_ref, p_ref,
                 vng_ref, vnb_ref, ws_ref, bst_ref, wpa_ref, wpb_ref, wout_ref, wpg_ref,
                 wpe_ref, lng_ref, lnb_ref, o_ref, vn_ref, ya_ref):
    row = lax.broadcasted_iota(jnp.int32, (CHUNK, CHUNK), 0)
    col = lax.broadcasted_iota(jnp.int32, (CHUNK, CHUNK), 1)
    causal = col <= row
    for r0 in range(0, TAIL_TM, TAIL_SUB):
        rows = slice(r0, r0 + TAIL_SUB)
        vn_ref[rows, :] = _layer_norm(
            v_ref[rows, :].astype(F32), vng_ref[...], vnb_ref[...]).astype(BF16)
        for g in range(A_GROUPS):
            cs = slice(g * LANES, (g + 1) * LANES)
            wsg = jnp.where(causal, ws_ref[g], jnp.zeros((CHUNK, CHUNK), BF16))
            bias = bst_ref[:, g:g + 1]
            for c0 in range(r0, r0 + TAIL_SUB, CHUNK):
                rs = slice(c0, c0 + CHUNK)
                mixed = _dot(wsg, vn_ref[rs, cs]) + bias
                ga = ga_ref[rs, cs].astype(F32)
                ya = u_ref[rs, cs].astype(F32) * mixed * (ga * jax.nn.sigmoid(ga))
                ya_ref[rs, cs] = ya.astype(BF16)
        ya_p = _dot(ya_ref[rows, :], wpa_ref[...])
        yb_p = _dot(yb_ref[rows, :], wpb_ref[...])
        merged = (jax.nn.sigmoid(ma_ref[rows, :].astype(F32)) * ya_p
                  + jax.nn.sigmoid(mb_ref[rows, :].astype(F32)) * yb_p)
        y = _dot(merged.astype(BF16), wout_ref[...])
        h = ALPHA * x_ref[rows, :] + y
        gate = jax.nn.sigmoid(_dot(h.astype(BF16), wpg_ref[...]))
        pe = _dot(p_ref[rows, :].astype(BF16), wpe_ref[...])
        h = h + pe * gate
        o_ref[rows, :] = _layer_norm(h, lng_ref[...], lnb_ref[...])


def _tail(proj, yb, x2d, p3d, layer, vn_g, vn_b, ws, bst, w_pa, w_pb, w_out, w_pg, w_pe,
          ln_g, ln_b):
    n = x2d.shape[0]
    tm = TAIL_TM

    def seg(c):
        return pl.BlockSpec((tm, D_MODEL), lambda i, c=c: (i, c))

    def const(shape):
        return _layer_spec(layer, shape)

    return pl.pallas_call(
        _tail_kernel,
        out_shape=jax.ShapeDtypeStruct((n, D_MODEL), F32),
        grid=(n // tm,),
        in_specs=[
            seg(COL_U), seg(COL_V), seg(COL_GA), seg(COL_MA), seg(COL_MB),
            seg(0), seg(0),
            pl.BlockSpec((None, tm, PLE_DIM), lambda i: (layer, i, 0)),
            const((1, D_MODEL)), const((1, D_MODEL)),
            const((A_GROUPS, CHUNK, CHUNK)), const((CHUNK, A_GROUPS)),
            const((D_MODEL, D_MODEL)), const((D_MODEL, D_MODEL)),
            const((D_MODEL, D_MODEL)), const((D_MODEL, D_MODEL)),
            const((PLE_DIM, D_MODEL)),
            const((1, D_MODEL)), const((1, D_MODEL)),
        ],
        out_specs=pl.BlockSpec((tm, D_MODEL), lambda i: (i, 0)),
        scratch_shapes=[pltpu.VMEM((tm, D_MODEL), BF16), pltpu.VMEM((tm, D_MODEL), BF16)],
        compiler_params=pltpu.CompilerParams(
            dimension_semantics=("arbitrary",), vmem_limit_bytes=VMEM_LIMIT),
        name="tail",
    )(proj, proj, proj, proj, proj, yb, x2d, p3d, vn_g, vn_b, ws, bst,
      w_pa, w_pb, w_out, w_pg, w_pe, ln_g, ln_b)


def kernel(x, p, w_in, vn_g, vn_b, w_s, b_s, w_pa, w_pb, w_out, w_pe, w_pg, ln_g, ln_b):
    batch, seq, d = x.shape
    n = batch * seq
    assert d == D_MODEL and seq % (ATT_T * ATT_QT) == 0 and n % PROJ_TM == 0 and n % TAIL_TM == 0
    assert PROJ_TN == D_MODEL
    depth = w_in.shape[0]
    xf = x.reshape(n, d)
    p3d = p.reshape(depth, n, PLE_DIM)
    row = lambda a: a.reshape(depth, 1, d)
    w_s, w_pa, w_pb, w_out, w_pg, w_pe = (
        a.astype(BF16) for a in (w_s, w_pa, w_pb, w_out, w_pg, w_pe))
    vn_g, vn_b, ln_g, ln_b = row(vn_g), row(vn_b), row(ln_g), row(ln_b)
    bst = jnp.swapaxes(b_s, 1, 2)
    for i in range(depth):
        proj = _in_proj(xf, w_in, i)
        yb = _attention(proj, batch, seq)
        xf = _tail(proj, yb, xf, p3d, i, vn_g, vn_b, w_s, bst, w_pa, w_pb, w_out, w_pg, w_pe,
                   ln_g, ln_b)
    return xf.reshape(batch, seq, d)
```

```python
import math

import jax
import jax.numpy as jnp
from jax import lax
from jax.experimental import pallas as pl
from jax.experimental.pallas import tpu as pltpu

F32 = jnp.float32
BF16 = jnp.bfloat16

D_MODEL = 1024
DEPTH = 4
PLE_DIM = 256
CHUNK = 128
A_GROUPS = 8
HEAD_DIM = 64
LANES = 128
IN_WIDTH = 9 * D_MODEL
ALPHA = (2 * DEPTH) ** 0.25
LN_EPS = 1e-5
LOG2E = math.log2(math.e)
Q_SCALE = HEAD_DIM ** -0.5 * LOG2E

COL_U, COL_V, COL_GA, COL_Q, COL_K, COL_VB, COL_GB, COL_MA, COL_MB = range(9)

PROJ_TM = 2048
PROJ_TN = 1536
ATT_T = 256
ATT_LG = 8
ATT_QT = 2
ATT_SKIP_LOG2 = -150.0
TAIL_TM = 512
TAIL_SUB = 256
VMEM_LIMIT = 60 * 1024 * 1024


def _dot(a, b):
    return jnp.dot(a, b, preferred_element_type=F32)


def _dot_nt(a, b):
    return lax.dot_general(a, b, (((1,), (1,)), ((), ())), preferred_element_type=F32)


def _layer_spec(layer, shape):
    return pl.BlockSpec((None,) + shape, lambda *_: (layer,) + (0,) * len(shape))


def _in_proj_kernel(x_ref, w_ref, o_ref):
    acc = _dot(x_ref[...].astype(BF16), w_ref[...].astype(BF16))
    col = pl.program_id(1) * PROJ_TN + lax.broadcasted_iota(jnp.int32, (1, PROJ_TN), 1)
    is_q = jnp.logical_and(col >= COL_Q * D_MODEL, col < (COL_Q + 1) * D_MODEL)
    o_ref[...] = (acc * jnp.where(is_q, Q_SCALE, 1.0)).astype(BF16)


def _in_proj(x2d, w_in, layer):
    n = x2d.shape[0]
    return pl.pallas_call(
        _in_proj_kernel,
        out_shape=jax.ShapeDtypeStruct((n, IN_WIDTH), BF16),
        grid=(n // PROJ_TM, IN_WIDTH // PROJ_TN),
        in_specs=[
            pl.BlockSpec((PROJ_TM, D_MODEL), lambda i, j: (i, 0)),
            pl.BlockSpec((None, D_MODEL, PROJ_TN), lambda i, j: (layer, 0, j)),
        ],
        out_specs=pl.BlockSpec((PROJ_TM, PROJ_TN), lambda i, j: (i, j)),
        compiler_params=pltpu.CompilerParams(
            dimension_semantics=("arbitrary", "arbitrary"), vmem_limit_bytes=VMEM_LIMIT),
        name="in_proj",
    )(x2d, w_in)


def _attn_block(q2, kb, vb, carry, upper, mask):
    s = _dot_nt(q2, kb)
    sp = jnp.maximum(s, jnp.log(1.0 + jnp.exp2(jnp.minimum(s, 64.0))) * LOG2E)
    spm = sp if mask is None else jnp.where(mask, sp, 0.0)
    later = _dot(spm.astype(BF16), upper)
    w = jnp.exp2((s - sp) + (carry - later))
    if mask is not None:
        w = jnp.where(mask, w, 0.0)
    pv = _dot(w.astype(BF16), vb)
    return pv, carry - (later[:, 0:1] + spm[:, 0:1])


def _attn_kernel(q_ref, k_ref, v_ref, g_ref, o_ref, acc_ref, carry_ref):
    t = ATT_T
    step = pl.program_id(2)
    lane = lax.broadcasted_iota(jnp.int32, (t, LANES), 1)
    head0 = lane < HEAD_DIM
    row = lax.broadcasted_iota(jnp.int32, (t, t), 0)
    col = lax.broadcasted_iota(jnp.int32, (t, t), 1)
    upper = (row > col).astype(BF16)
    strict1 = col < row
    strict = jnp.concatenate([strict1, strict1], axis=0)

    def lanes(g):
        return slice(g * LANES, (g + 1) * LANES)

    def tile(ref, idx, g):
        return ref[pl.ds(pl.multiple_of(idx * t, t), t), lanes(g)]

    q2 = {}
    m = None
    for u in range(ATT_QT):
        qi = step * ATT_QT + u
        rows = slice(u * t, (u + 1) * t)
        prev = jnp.maximum(qi - 1, 0)
        has_prev = qi > 0
        for g in range(ATT_LG):
            q = q_ref[rows, lanes(g)]
            zero = jnp.zeros_like(q)
            q2[u, g] = jnp.concatenate(
                [jnp.where(head0, q, zero), jnp.where(head0, zero, q)], axis=0)
            pv0, c0 = _attn_block(q2[u, g], tile(k_ref, qi, g), tile(v_ref, qi, g),
                                  jnp.zeros((2 * t, 1), F32), upper, strict)
            c0 = jnp.where(has_prev, c0, -jnp.inf)
            pv1, c1 = _attn_block(q2[u, g], tile(k_ref, prev, g), tile(v_ref, prev, g),
                                  c0, upper, None)
            acc_ref[u, g] = pv0 + pv1
            carry_ref[u, g] = c1
            cm = jnp.max(c1)
            m = cm if m is None else jnp.maximum(m, cm)

    def cond(state):
        j, worst = state
        return jnp.logical_and((step * ATT_QT + ATT_QT - 1) - 2 - j >= 0, worst > ATT_SKIP_LOG2)

    def body(state):
        j, _ = state
        worst = None
        for u in range(ATT_QT):
            idx = step * ATT_QT + u - 2 - j
            live = jnp.where(idx >= 0, 0.0, -jnp.inf)
            idx = jnp.maximum(idx, 0)
            for g in range(ATT_LG):
                pv, c = _attn_block(q2[u, g], tile(k_ref, idx, g), tile(v_ref, idx, g),
                                    carry_ref[u, g] + live, upper, None)
                acc_ref[u, g] += pv
                carry_ref[u, g] = c
                cm = jnp.max(c)
                worst = cm if worst is None else jnp.maximum(worst, cm)
        return j + 1, worst

    lax.while_loop(cond, body, (0, m))

    for u in range(ATT_QT):
        rows = slice(u * t, (u + 1) * t)
        for g in range(ATT_LG):
            o = jnp.where(head0, acc_ref[u, g, 0:t, :], acc_ref[u, g, t:2 * t, :])
            gate = g_ref[rows, lanes(g)].astype(F32)
            o_ref[rows, lanes(g)] = (o * (gate * jax.nn.sigmoid(gate))).astype(BF16)


def _attention(proj, batch, seq):
    n = proj.shape[0]
    t = ATT_T * ATT_QT
    qt = seq // t
    w = ATT_LG * LANES
    hg = D_MODEL // w
    cpb = D_MODEL // w
    return pl.pallas_call(
        _attn_kernel,
        out_shape=jax.ShapeDtypeStruct((n, D_MODEL), BF16),
        grid=(batch, hg, qt),
        in_specs=[
            pl.BlockSpec((t, w), lambda b, h, i: (b * qt + i, COL_Q * cpb + h)),
            pl.BlockSpec((seq, w), lambda b, h, i: (b, COL_K * cpb + h)),
            pl.BlockSpec((seq, w), lambda b, h, i: (b, COL_VB * cpb + h)),
            pl.BlockSpec((t, w), lambda b, h, i: (b * qt + i, COL_GB * cpb + h)),
        ],
        out_specs=pl.BlockSpec((t, w), lambda b, h, i: (b * qt + i, h)),
        scratch_shapes=[pltpu.VMEM((ATT_QT, ATT_LG, 2 * ATT_T, LANES), F32),
                        pltpu.VMEM((ATT_QT, ATT_LG, 2 * ATT_T, 1), F32)],
        compiler_params=pltpu.CompilerParams(
            dimension_semantics=("arbitrary", "arbitrary", "arbitrary"),
            vmem_limit_bytes=VMEM_LIMIT),
        name="attention",
    )(proj, proj, proj, proj)


def _layer_norm(x, g, b):
    mu = jnp.mean(x, axis=-1, keepdims=True)
    d = x - mu
    var = jnp.mean(d * d, axis=-1, keepdims=True)
    return d * lax.rsqrt(var + LN_EPS) * g + b


def _tail_kernel(u_ref, v_ref, ga_ref, ma_ref, mb_ref, yb_ref, x_ref, p_ref,
                 vng_ref, vnb_ref, ws_ref, bst_ref, wpa_ref, wpb_ref, wout_ref, wpg_ref,
                 wpe_ref, lng_ref, lnb_ref, o_ref, vn_ref, ya_ref):
    row = lax.broadcasted_iota(jnp.int32, (CHUNK, CHUNK), 0)
    col = lax.broadcasted_iota(jnp.int32, (CHUNK, CHUNK), 1)
    causal = col <= row
    for r0 in range(0, TAIL_TM, TAIL_SUB):
        rows = slice(r0, r0 + TAIL_SUB)
        vn_ref[rows, :] = _layer_norm(
            v_ref[rows, :].astype(F32), vng_ref[...], vnb_ref[...]).astype(BF16)
        for g in range(A_GROUPS):
            cs = slice(g * LANES, (g + 1) * LANES)
            wsg = jnp.where(causal, ws_ref[g], jnp.zeros((CHUNK, CHUNK), BF16))
            bias = bst_ref[:, g:g + 1]
            for c0 in range(r0, r0 + TAIL_SUB, CHUNK):
                rs = slice(c0, c0 + CHUNK)
                mixed = _dot(wsg, vn_ref[rs, cs]) + bias
                ga = ga_ref[rs, cs].astype(F32)
                ya = u_ref[rs, cs].astype(F32) * mixed * (ga * jax.nn.sigmoid(ga))
                ya_ref[rs, cs] = ya.astype(BF16)
        ya_p = _dot(ya_ref[rows, :], wpa_ref[...])
        yb_p = _dot(yb_ref[rows, :], wpb_ref[...])
        merged = (jax.nn.sigmoid(ma_ref[rows, :].astype(F32)) * ya_p
                  + jax.nn.sigmoid(mb_ref[rows, :].astype(F32)) * yb_p)
        y = _dot(merged.astype(BF16), wout_ref[...])
        h = ALPHA * x_ref[rows, :] + y
        gate = jax.nn.sigmoid(_dot(h.astype(BF16), wpg_ref[...]))
        pe = _dot(p_ref[rows, :].astype(BF16), wpe_ref[...])
        h = h + pe * gate
        o_ref[rows, :] = _layer_norm(h, lng_ref[...], lnb_ref[...])


def _tail(proj, yb, x2d, p3d, layer, vn_g, vn_b, ws, bst, w_pa, w_pb, w_out, w_pg, w_pe,
          ln_g, ln_b):
    n = x2d.shape[0]
    tm = TAIL_TM

    def seg(c):
        return pl.BlockSpec((tm, D_MODEL), lambda i, c=c: (i, c))

    def const(shape):
        return _layer_spec(layer, shape)

    return pl.pallas_call(
        _tail_kernel,
        out_shape=jax.ShapeDtypeStruct((n, D_MODEL), F32),
        grid=(n // tm,),
        in_specs=[
            seg(COL_U), seg(COL_V), seg(COL_GA), seg(COL_MA), seg(COL_MB),
            seg(0), seg(0),
            pl.BlockSpec((None, tm, PLE_DIM), lambda i: (layer, i, 0)),
            const((1, D_MODEL)), const((1, D_MODEL)),
            const((A_GROUPS, CHUNK, CHUNK)), const((CHUNK, A_GROUPS)),
            const((D_MODEL, D_MODEL)), const((D_MODEL, D_MODEL)),
            const((D_MODEL, D_MODEL)), const((D_MODEL, D_MODEL)),
            const((PLE_DIM, D_MODEL)),
            const((1, D_MODEL)), const((1, D_MODEL)),
        ],
        out_specs=pl.BlockSpec((tm, D_MODEL), lambda i: (i, 0)),
        scratch_shapes=[pltpu.VMEM((tm, D_MODEL), BF16), pltpu.VMEM((tm, D_MODEL), BF16)],
        compiler_params=pltpu.CompilerParams(
            dimension_semantics=("arbitrary",), vmem_limit_bytes=VMEM_LIMIT),
        name="tail",
    )(proj, proj, proj, proj, proj, yb, x2d, p3d, vn_g, vn_b, ws, bst,
      w_pa, w_pb, w_out, w_pg, w_pe, ln_g, ln_b)


def kernel(x, p, w_in, vn_g, vn_b, w_s, b_s, w_pa, w_pb, w_out, w_pe, w_pg, ln_g, ln_b):
    batch, seq, d = x.shape
    n = batch * seq
    assert d == D_MODEL and seq % (ATT_T * ATT_QT) == 0 and n % PROJ_TM == 0 and n % TAIL_TM == 0
    assert IN_WIDTH % PROJ_TN == 0
    depth = w_in.shape[0]
    xf = x.reshape(n, d)
    p3d = p.reshape(depth, n, PLE_DIM)
    row = lambda a: a.reshape(depth, 1, d)
    w_s, w_pa, w_pb, w_out, w_pg, w_pe = (
        a.astype(BF16) for a in (w_s, w_pa, w_pb, w_out, w_pg, w_pe))
    vn_g, vn_b, ln_g, ln_b = row(vn_g), row(vn_b), row(ln_g), row(ln_b)
    bst = jnp.swapaxes(b_s, 1, 2)
    for i in range(depth):
        proj = _in_proj(xf, w_in, i)
        yb = _attention(proj, batch, seq)
        xf = _tail(proj, yb, xf, p3d, i, vn_g, vn_b, w_s, bst, w_pa, w_pb, w_out, w_pg, w_pe,
                   ln_g, ln_b)
    return xf.reshape(batch, seq, d)
```

```python
import math

import jax
import jax.numpy as jnp
from jax import lax
from jax.experimental import pallas as pl
from jax.experimental.pallas import tpu as pltpu

F32 = jnp.float32
BF16 = jnp.bfloat16

D_MODEL = 1024
DEPTH = 4
PLE_DIM = 256
CHUNK = 128
A_GROUPS = 8
HEAD_DIM = 64
LANES = 128
IN_WIDTH = 9 * D_MODEL
ALPHA = (2 * DEPTH) ** 0.25
LN_EPS = 1e-5
LOG2E = math.log2(math.e)
Q_SCALE = HEAD_DIM ** -0.5 * LOG2E

COL_U, COL_V, COL_GA, COL_Q, COL_K, COL_VB, COL_GB, COL_MA, COL_MB = range(9)

PROJ_TM = 2048
PROJ_TN = 1536
ATT_T = 256
ATT_LG = 8
ATT_QT = 2
ATT_SKIP_LOG2 = -150.0
TAIL_TM = 512
TAIL_SUB = 256
VMEM_LIMIT = 60 * 1024 * 1024


def _dot(a, b):
    return jnp.dot(a, b, preferred_element_type=F32)


def _dot_nt(a, b):
    return lax.dot_general(a, b, (((1,), (1,)), ((), ())), preferred_element_type=F32)


def _layer_spec(layer, shape):
    return pl.BlockSpec((None,) + shape, lambda *_: (layer,) + (0,) * len(shape))


def _in_proj_kernel(x_ref, w_ref, o_ref):
    acc = _dot(x_ref[...].astype(BF16), w_ref[...].astype(BF16))
    col = pl.program_id(1) * PROJ_TN + lax.broadcasted_iota(jnp.int32, (1, PROJ_TN), 1)
    is_q = jnp.logical_and(col >= COL_Q * D_MODEL, col < (COL_Q + 1) * D_MODEL)
    o_ref[...] = (acc * jnp.where(is_q, Q_SCALE, 1.0)).astype(BF16)


def _in_proj(x2d, w_in, layer):
    n = x2d.shape[0]
    return pl.pallas_call(
        _in_proj_kernel,
        out_shape=jax.ShapeDtypeStruct((n, IN_WIDTH), BF16),
        grid=(n // PROJ_TM, IN_WIDTH // PROJ_TN),
        in_specs=[
            pl.BlockSpec((PROJ_TM, D_MODEL), lambda i, j: (i, 0)),
            pl.BlockSpec((None, D_MODEL, PROJ_TN), lambda i, j: (layer, 0, j)),
        ],
        out_specs=pl.BlockSpec((PROJ_TM, PROJ_TN), lambda i, j: (i, j)),
        compiler_params=pltpu.CompilerParams(
            dimension_semantics=("arbitrary", "arbitrary"), vmem_limit_bytes=VMEM_LIMIT),
        name="in_proj",
    )(x2d, w_in)


def _attn_block(q2, kb, vb, carry, upper, mask):
    s = _dot_nt(q2, kb)
    sp = jnp.maximum(s, jnp.log(1.0 + jnp.exp2(jnp.minimum(s, 64.0))) * LOG2E)
    spm = sp if mask is None else jnp.where(mask, sp, 0.0)
    later = _dot(spm.astype(BF16), upper)
    w = jnp.exp2((s - sp) + (carry - later))
    if mask is not None:
        w = jnp.where(mask, w, 0.0)
    pv = _dot(w.astype(BF16), vb)
    return pv, carry - (later[:, 0:1] + spm[:, 0:1])


def _attn_kernel(q_ref, k_ref, v_ref, g_ref, o_ref, acc_ref, carry_ref):
    t = ATT_T
    step = pl.program_id(2)
    lane = lax.broadcasted_iota(jnp.int32, (t, LANES), 1)
    head0 = lane < HEAD_DIM
    row = lax.broadcasted_iota(jnp.int32, (t, t), 0)
    col = lax.broadcasted_iota(jnp.int32, (t, t), 1)
    upper = (row > col).astype(BF16)
    strict1 = col < row
    strict = jnp.concatenate([strict1, strict1], axis=0)

    def lanes(g):
        return slice(g * LANES, (g + 1) * LANES)

    def tile(ref, idx, g):
        return ref[pl.ds(pl.multiple_of(idx * t, t), t), lanes(g)]

    q2 = {}
    m = None
    for u in range(ATT_QT):
        qi = step * ATT_QT + u
        rows = slice(u * t, (u + 1) * t)
        prev = jnp.maximum(qi - 1, 0)
        has_prev = qi > 0
        for g in range(ATT_LG):
            q = q_ref[rows, lanes(g)]
            zero = jnp.zeros_like(q)
            q2[u, g] = jnp.concatenate(
                [jnp.where(head0, q, zero), jnp.where(head0, zero, q)], axis=0)
            pv0, c0 = _attn_block(q2[u, g], tile(k_ref, qi, g), tile(v_ref, qi, g),
                                  jnp.zeros((2 * t, 1), F32), upper, strict)
            c0 = jnp.where(has_prev, c0, -jnp.inf)
            pv1, c1 = _attn_block(q2[u, g], tile(k_ref, prev, g), tile(v_ref, prev, g),
                                  c0, upper, None)
            acc_ref[u, g] = pv0 + pv1
            carry_ref[u, g] = c1
            cm = jnp.max(c1)
            m = cm if m is None else jnp.maximum(m, cm)

    def cond(state):
        j, worst = state
        return jnp.logical_and((step * ATT_QT + ATT_QT - 1) - 2 - j >= 0, worst > ATT_SKIP_LOG2)

    def body(state):
        j, _ = state
        worst = None
        for u in range(ATT_QT):
            idx = step * ATT_QT + u - 2 - j
            live = jnp.where(idx >= 0, 0.0, -jnp.inf)
            idx = jnp.maximum(idx, 0)
            for g in range(ATT_LG):
                pv, c = _attn_block(q2[u, g], tile(k_ref, idx, g), tile(v_ref, idx, g),
                                    carry_ref[u, g] + live, upper, None)
                acc_ref[u, g] += pv
                carry_ref[u, g] = c
                cm = jnp.max(c)
                worst = cm if worst is None else jnp.maximum(worst, cm)
        return j + 1, worst

    lax.while_loop(cond, body, (0, m))

    for u in range(ATT_QT):
        rows = slice(u * t, (u + 1) * t)
        for g in range(ATT_LG):
            o = jnp.where(head0, acc_ref[u, g, 0:t, :], acc_ref[u, g, t:2 * t, :])
            gate = g_ref[rows, lanes(g)].astype(F32)
            o_ref[rows, lanes(g)] = (o * (gate * jax.nn.sigmoid(gate))).astype(BF16)


def _attention(proj, batch, seq):
    n = proj.shape[0]
    t = ATT_T * ATT_QT
    qt = seq // t
    w = ATT_LG * LANES
    hg = D_MODEL // w
    cpb = D_MODEL // w
    return pl.pallas_call(
        _attn_kernel,
        out_shape=jax.ShapeDtypeStruct((n, D_MODEL), BF16),
        grid=(batch, hg, qt),
        in_specs=[
            pl.BlockSpec((t, w), lambda b, h, i: (b * qt + i, COL_Q * cpb + h)),
            pl.BlockSpec((seq, w), lambda b, h, i: (b, COL_K * cpb + h)),
            pl.BlockSpec((seq, w), lambda b, h, i: (b, COL_VB * cpb + h)),
            pl.BlockSpec((t, w), lambda b, h, i: (b * qt + i, COL_GB * cpb + h)),
        ],
        out_specs=pl.BlockSpec((t, w), lambda b, h, i: (b * qt + i, h)),
        scratch_shapes=[pltpu.VMEM((ATT_QT, ATT_LG, 2 * ATT_T, LANES), F32),
                        pltpu.VMEM((ATT_QT, ATT_LG, 2 * ATT_T, 1), F32)],
        compiler_params=pltpu.CompilerParams(
            dimension_semantics=("arbitrary", "arbitrary", "arbitrary"),
            vmem_limit_bytes=VMEM_LIMIT),
        name="attention",
    )(proj, proj, proj, proj)


def _layer_norm(x, g, b):
    mu = jnp.mean(x, axis=-1, keepdims=True)
    d = x - mu
    var = jnp.mean(d * d, axis=-1, keepdims=True)
    return d * lax.rsqrt(var + LN_EPS) * g + b


def _tail_kernel(u_ref, v_ref, ga_ref, ma_ref, mb_ref, yb_ref, x_ref, p_ref,
                 vng_ref, vnb_ref, ws_ref, bst_ref, wpa_ref, wpb_ref, wout_ref, wpg_ref,
                 wpe_ref, lng_ref, lnb_ref, o_ref, vn_ref, ya_ref):
    row = lax.broadcasted_iota(jnp.int32, (CHUNK, CHUNK), 0)
    col = lax.broadcasted_iota(jnp.int32, (CHUNK, CHUNK), 1)
    causal = col <= row
    for r0 in range(0, TAIL_TM, TAIL_SUB):
        rows = slice(r0, r0 + TAIL_SUB)
        yb_p = _dot(yb_ref[rows, :], wpb_ref[...])
        vn_ref[rows, :] = _layer_norm(
            v_ref[rows, :].astype(F32), vng_ref[...], vnb_ref[...]).astype(BF16)
        for g in range(A_GROUPS):
            cs = slice(g * LANES, (g + 1) * LANES)
            wsg = jnp.where(causal, ws_ref[g], jnp.zeros((CHUNK, CHUNK), BF16))
            bias = bst_ref[:, g:g + 1]
            for c0 in range(r0, r0 + TAIL_SUB, CHUNK):
                rs = slice(c0, c0 + CHUNK)
                mixed = _dot(wsg, vn_ref[rs, cs]) + bias
                ga = ga_ref[rs, cs].astype(F32)
                ya = u_ref[rs, cs].astype(F32) * mixed * (ga * jax.nn.sigmoid(ga))
                ya_ref[rs, cs] = ya.astype(BF16)
        ya_p = _dot(ya_ref[rows, :], wpa_ref[...])
        merged = (jax.nn.sigmoid(ma_ref[rows, :].astype(F32)) * ya_p
                  + jax.nn.sigmoid(mb_ref[rows, :].astype(F32)) * yb_p)
        y = _dot(merged.astype(BF16), wout_ref[...])
        h = ALPHA * x_ref[rows, :] + y
        gate = jax.nn.sigmoid(_dot(h.astype(BF16), wpg_ref[...]))
        pe = _dot(p_ref[rows, :].astype(BF16), wpe_ref[...])
        h = h + pe * gate
        o_ref[rows, :] = _layer_norm(h, lng_ref[...], lnb_ref[...])


def _tail(proj, yb, x2d, p3d, layer, vn_g, vn_b, ws, bst, w_pa, w_pb, w_out, w_pg, w_pe,
          ln_g, ln_b):
    n = x2d.shape[0]
    tm = TAIL_TM

    def seg(c):
        return pl.BlockSpec((tm, D_MODEL), lambda i, c=c: (i, c))

    def const(shape):
        return _layer_spec(layer, shape)

    return pl.pallas_call(
        _tail_kernel,
        out_shape=jax.ShapeDtypeStruct((n, D_MODEL), F32),
        grid=(n // tm,),
        in_specs=[
            seg(COL_U), seg(COL_V), seg(COL_GA), seg(COL_MA), seg(COL_MB),
            seg(0), seg(0),
            pl.BlockSpec((None, tm, PLE_DIM), lambda i: (layer, i, 0)),
            const((1, D_MODEL)), const((1, D_MODEL)),
            const((A_GROUPS, CHUNK, CHUNK)), const((CHUNK, A_GROUPS)),
            const((D_MODEL, D_MODEL)), const((D_MODEL, D_MODEL)),
            const((D_MODEL, D_MODEL)), const((D_MODEL, D_MODEL)),
            const((PLE_DIM, D_MODEL)),
            const((1, D_MODEL)), const((1, D_MODEL)),
        ],
        out_specs=pl.BlockSpec((tm, D_MODEL), lambda i: (i, 0)),
        scratch_shapes=[pltpu.VMEM((tm, D_MODEL), BF16), pltpu.VMEM((tm, D_MODEL), BF16)],
        compiler_params=pltpu.CompilerParams(
            dimension_semantics=("arbitrary",), vmem_limit_bytes=VMEM_LIMIT),
        name="tail",
    )(proj, proj, proj, proj, proj, yb, x2d, p3d, vn_g, vn_b, ws, bst,
      w_pa, w_pb, w_out, w_pg, w_pe, ln_g, ln_b)


def kernel(x, p, w_in, vn_g, vn_b, w_s, b_s, w_pa, w_pb, w_out, w_pe, w_pg, ln_g, ln_b):
    batch, seq, d = x.shape
    n = batch * seq
    assert d == D_MODEL and seq % (ATT_T * ATT_QT) == 0 and n % PROJ_TM == 0 and n % TAIL_TM == 0
    assert IN_WIDTH % PROJ_TN == 0
    depth = w_in.shape[0]
    xf = x.reshape(n, d)
    p3d = p.reshape(depth, n, PLE_DIM)
    row = lambda a: a.reshape(depth, 1, d)
    w_s, w_pa, w_pb, w_out, w_pg, w_pe = (
        a.astype(BF16) for a in (w_s, w_pa, w_pb, w_out, w_pg, w_pe))
    vn_g, vn_b, ln_g, ln_b = row(vn_g), row(vn_b), row(ln_g), row(ln_b)
    bst = jnp.swapaxes(b_s, 1, 2)
    for i in range(depth):
        proj = _in_proj(xf, w_in, i)
        yb = _attention(proj, batch, seq)
        xf = _tail(proj, yb, xf, p3d, i, vn_g, vn_b, w_s, bst, w_pa, w_pb, w_out, w_pg, w_pe,
                   ln_g, ln_b)
    return xf.reshape(batch, seq, d)
```

```python
import math

import jax
import jax.numpy as jnp
from jax import lax
from jax.experimental import pallas as pl
from jax.experimental.pallas import tpu as pltpu

F32 = jnp.float32
BF16 = jnp.bfloat16

D_MODEL = 1024
DEPTH = 4
PLE_DIM = 256
CHUNK = 128
A_GROUPS = 8
HEAD_DIM = 64
LANES = 128
IN_WIDTH = 9 * D_MODEL
ALPHA = (2 * DEPTH) ** 0.25
LN_EPS = 1e-5
LOG2E = math.log2(math.e)
Q_SCALE = HEAD_DIM ** -0.5 * LOG2E

COL_U, COL_V, COL_GA, COL_Q, COL_K, COL_VB, COL_GB, COL_MA, COL_MB = range(9)

PROJ_TM = 2048
PROJ_TN = 1536
ATT_T = 256
ATT_LG = 8
ATT_QT = 2
ATT_SKIP_LOG2 = -150.0
TAIL_TM = 512
TAIL_SUB = 256
V7X_VMEM_BYTES = 64 * 1024 * 1024
VMEM_LIMIT = V7X_VMEM_BYTES - 4 * 1024 * 1024
SOFTPLUS_CLAMP_LOG2 = 64.0


def _dot(a, b):
    return jnp.dot(a, b, preferred_element_type=F32)


def _dot_nt(a, b):
    return lax.dot_general(a, b, (((1,), (1,)), ((), ())), preferred_element_type=F32)


def _layer_spec(layer, shape):
    return pl.BlockSpec((None,) + shape, lambda *_: (layer,) + (0,) * len(shape))


def _in_proj_kernel(x_ref, w_ref, o_ref):
    acc = _dot(x_ref[...].astype(BF16), w_ref[...].astype(BF16))
    col = pl.program_id(1) * PROJ_TN + lax.broadcasted_iota(jnp.int32, (1, PROJ_TN), 1)
    is_q = jnp.logical_and(col >= COL_Q * D_MODEL, col < (COL_Q + 1) * D_MODEL)
    o_ref[...] = (acc * jnp.where(is_q, Q_SCALE, 1.0)).astype(BF16)


def _in_proj(x2d, w_in, layer):
    n = x2d.shape[0]
    return pl.pallas_call(
        _in_proj_kernel,
        out_shape=jax.ShapeDtypeStruct((n, IN_WIDTH), BF16),
        grid=(n // PROJ_TM, IN_WIDTH // PROJ_TN),
        in_specs=[
            pl.BlockSpec((PROJ_TM, D_MODEL), lambda i, j: (i, 0)),
            pl.BlockSpec((None, D_MODEL, PROJ_TN), lambda i, j: (layer, 0, j)),
        ],
        out_specs=pl.BlockSpec((PROJ_TM, PROJ_TN), lambda i, j: (i, j)),
        compiler_params=pltpu.CompilerParams(
            dimension_semantics=("arbitrary", "arbitrary"), vmem_limit_bytes=VMEM_LIMIT),
        name="in_proj",
    )(x2d, w_in)


def _attn_block(q2, kb, vb, carry, upper, mask):
    s = _dot_nt(q2, kb)
    sp = jnp.maximum(s, jnp.log(1.0 + jnp.exp2(jnp.minimum(s, SOFTPLUS_CLAMP_LOG2))) * LOG2E)
    spm = sp if mask is None else jnp.where(mask, sp, 0.0)
    later = _dot(spm.astype(BF16), upper)
    w = jnp.exp2((s - sp) + (carry - later))
    if mask is not None:
        w = jnp.where(mask, w, 0.0)
    pv = _dot(w.astype(BF16), vb)
    return pv, carry - (later[:, 0:1] + spm[:, 0:1])


def _attn_kernel(q_ref, k_ref, v_ref, g_ref, o_ref, acc_ref, carry_ref):
    t = ATT_T
    step = pl.program_id(2)
    lane = lax.broadcasted_iota(jnp.int32, (t, LANES), 1)
    head0 = lane < HEAD_DIM
    row = lax.broadcasted_iota(jnp.int32, (t, t), 0)
    col = lax.broadcasted_iota(jnp.int32, (t, t), 1)
    upper = (row > col).astype(BF16)
    strict1 = col < row
    strict = jnp.concatenate([strict1, strict1], axis=0)

    def lanes(g):
        return slice(g * LANES, (g + 1) * LANES)

    def tile(ref, idx, g):
        return ref[pl.ds(pl.multiple_of(idx * t, t), t), lanes(g)]

    q2 = {}
    m = None
    for u in range(ATT_QT):
        qi = step * ATT_QT + u
        rows = slice(u * t, (u + 1) * t)
        prev = jnp.maximum(qi - 1, 0)
        has_prev = qi > 0
        for g in range(ATT_LG):
            q = q_ref[rows, lanes(g)]
            zero = jnp.zeros_like(q)
            q2[u, g] = jnp.concatenate(
                [jnp.where(head0, q, zero), jnp.where(head0, zero, q)], axis=0)
            pv0, c0 = _attn_block(q2[u, g], tile(k_ref, qi, g), tile(v_ref, qi, g),
                                  jnp.zeros((2 * t, 1), F32), upper, strict)
            c0 = jnp.where(has_prev, c0, -jnp.inf)
            pv1, c1 = _attn_block(q2[u, g], tile(k_ref, prev, g), tile(v_ref, prev, g),
                                  c0, upper, None)
            acc_ref[u, g] = pv0 + pv1
            carry_ref[u, g] = c1
            cm = jnp.max(c1)
            m = cm if m is None else jnp.maximum(m, cm)

    def cond(state):
        j, worst = state
        return jnp.logical_and((step * ATT_QT + ATT_QT - 1) - 2 - j >= 0, worst > ATT_SKIP_LOG2)

    def body(state):
        j, _ = state
        worst = None
        for u in range(ATT_QT):
            idx = step * ATT_QT + u - 2 - j
            live = jnp.where(idx >= 0, 0.0, -jnp.inf)
            idx = jnp.maximum(idx, 0)
            for g in range(ATT_LG):
                pv, c = _attn_block(q2[u, g], tile(k_ref, idx, g), tile(v_ref, idx, g),
                                    carry_ref[u, g] + live, upper, None)
                acc_ref[u, g] += pv
                carry_ref[u, g] = c
                cm = jnp.max(c)
                worst = cm if worst is None else jnp.maximum(worst, cm)
        return j + 1, worst

    lax.while_loop(cond, body, (0, m))

    for u in range(ATT_QT):
        rows = slice(u * t, (u + 1) * t)
        for g in range(ATT_LG):
            o = jnp.where(head0, acc_ref[u, g, 0:t, :], acc_ref[u, g, t:2 * t, :])
            gate = g_ref[rows, lanes(g)].astype(F32)
            o_ref[rows, lanes(g)] = (o * (gate * jax.nn.sigmoid(gate))).astype(BF16)


def _attention(proj, batch, seq):
    n = proj.shape[0]
    t = ATT_T * ATT_QT
    qt = seq // t
    w = ATT_LG * LANES
    hg = D_MODEL // w
    cpb = D_MODEL // w
    return pl.pallas_call(
        _attn_kernel,
        out_shape=jax.ShapeDtypeStruct((n, D_MODEL), BF16),
        grid=(batch, hg, qt),
        in_specs=[
            pl.BlockSpec((t, w), lambda b, h, i: (b * qt + i, COL_Q * cpb + h)),
            pl.BlockSpec((seq, w), lambda b, h, i: (b, COL_K * cpb + h)),
            pl.BlockSpec((seq, w), lambda b, h, i: (b, COL_VB * cpb + h)),
            pl.BlockSpec((t, w), lambda b, h, i: (b * qt + i, COL_GB * cpb + h)),
        ],
        out_specs=pl.BlockSpec((t, w), lambda b, h, i: (b * qt + i, h)),
        scratch_shapes=[pltpu.VMEM((ATT_QT, ATT_LG, 2 * ATT_T, LANES), F32),
                        pltpu.VMEM((ATT_QT, ATT_LG, 2 * ATT_T, 1), F32)],
        compiler_params=pltpu.CompilerParams(
            dimension_semantics=("arbitrary", "arbitrary", "arbitrary"),
            vmem_limit_bytes=VMEM_LIMIT),
        name="attention",
    )(proj, proj, proj, proj)


def _layer_norm(x, g, b):
    mu = jnp.mean(x, axis=-1, keepdims=True)
    d = x - mu
    var = jnp.mean(d * d, axis=-1, keepdims=True)
    return d * lax.rsqrt(var + LN_EPS) * g + b


def _tail_kernel(u_ref, v_ref, ga_ref, ma_ref, mb_ref, yb_ref, x_ref, p_ref,
                 vng_ref, vnb_ref, ws_ref, bst_ref, wpa_ref, wpb_ref, wout_ref, wpg_ref,
                 wpe_ref, lng_ref, lnb_ref, o_ref, vn_ref, ya_ref):
    row = lax.broadcasted_iota(jnp.int32, (CHUNK, CHUNK), 0)
    col = lax.broadcasted_iota(jnp.int32, (CHUNK, CHUNK), 1)
    causal = col <= row
    for r0 in range(0, TAIL_TM, TAIL_SUB):
        rows = slice(r0, r0 + TAIL_SUB)
        yb_p = _dot(yb_ref[rows, :], wpb_ref[...])
        vn_ref[rows, :] = _layer_norm(
            v_ref[rows, :].astype(F32), vng_ref[...], vnb_ref[...]).astype(BF16)
        for g in range(A_GROUPS):
            cs = slice(g * LANES, (g + 1) * LANES)
            wsg = jnp.where(causal, ws_ref[g], jnp.zeros((CHUNK, CHUNK), BF16))
            bias = bst_ref[:, g:g + 1]
            for c0 in range(r0, r0 + TAIL_SUB, CHUNK):
                rs = slice(c0, c0 + CHUNK)
                mixed = _dot(wsg, vn_ref[rs, cs]) + bias
                ga = ga_ref[rs, cs].astype(F32)
                ya = u_ref[rs, cs].astype(F32) * mixed * (ga * jax.nn.sigmoid(ga))
                ya_ref[rs, cs] = ya.astype(BF16)
        ya_p = _dot(ya_ref[rows, :], wpa_ref[...])
        merged = (jax.nn.sigmoid(ma_ref[rows, :].astype(F32)) * ya_p
                  + jax.nn.sigmoid(mb_ref[rows, :].astype(F32)) * yb_p)
        y = _dot(merged.astype(BF16), wout_ref[...])
        h = ALPHA * x_ref[rows, :] + y
        gate = jax.nn.sigmoid(_dot(h.astype(BF16), wpg_ref[...]))
        pe = _dot(p_ref[rows, :].astype(BF16), wpe_ref[...])
        h = h + pe * gate
        o_ref[rows, :] = _layer_norm(h, lng_ref[...], lnb_ref[...])


def _tail(proj, yb, x2d, p3d, layer, vn_g, vn_b, ws, bst, w_pa, w_pb, w_out, w_pg, w_pe,
          ln_g, ln_b):
    n = x2d.shape[0]
    tm = TAIL_TM

    def seg(c):
        return pl.BlockSpec((tm, D_MODEL), lambda i, c=c: (i, c))

    def const(shape):
        return _layer_spec(layer, shape)

    return pl.pallas_call(
        _tail_kernel,
        out_shape=jax.ShapeDtypeStruct((n, D_MODEL), F32),
        grid=(n // tm,),
        in_specs=[
            seg(COL_U), seg(COL_V), seg(COL_GA), seg(COL_MA), seg(COL_MB),
            seg(0), seg(0),
            pl.BlockSpec((None, tm, PLE_DIM), lambda i: (layer, i, 0)),
            const((1, D_MODEL)), const((1, D_MODEL)),
            const((A_GROUPS, CHUNK, CHUNK)), const((CHUNK, A_GROUPS)),
            const((D_MODEL, D_MODEL)), const((D_MODEL, D_MODEL)),
            const((D_MODEL, D_MODEL)), const((D_MODEL, D_MODEL)),
            const((PLE_DIM, D_MODEL)),
            const((1, D_MODEL)), const((1, D_MODEL)),
        ],
        out_specs=pl.BlockSpec((tm, D_MODEL), lambda i: (i, 0)),
        scratch_shapes=[pltpu.VMEM((tm, D_MODEL), BF16), pltpu.VMEM((tm, D_MODEL), BF16)],
        compiler_params=pltpu.CompilerParams(
            dimension_semantics=("arbitrary",), vmem_limit_bytes=VMEM_LIMIT),
        name="tail",
    )(proj, proj, proj, proj, proj, yb, x2d, p3d, vn_g, vn_b, ws, bst,
      w_pa, w_pb, w_out, w_pg, w_pe, ln_g, ln_b)


def kernel(x, p, w_in, vn_g, vn_b, w_s, b_s, w_pa, w_pb, w_out, w_pe, w_pg, ln_g, ln_b):
    batch, seq, d = x.shape
    n = batch * seq
    assert d == D_MODEL and seq % (ATT_T * ATT_QT) == 0 and n % PROJ_TM == 0 and n % TAIL_TM == 0
    assert IN_WIDTH % PROJ_TN == 0
    depth = w_in.shape[0]
    assert depth == DEPTH
    xf = x.reshape(n, d)
    p3d = p.reshape(depth, n, PLE_DIM)
    row = lambda a: a.reshape(depth, 1, d)
    w_s, w_pa, w_pb, w_out, w_pg, w_pe = (
        a.astype(BF16) for a in (w_s, w_pa, w_pb, w_out, w_pg, w_pe))
    vn_g, vn_b, ln_g, ln_b = row(vn_g), row(vn_b), row(ln_g), row(ln_b)
    bst = jnp.swapaxes(b_s, 1, 2)
    for i in range(depth):
        proj = _in_proj(xf, w_in, i)
        yb = _attention(proj, batch, seq)
        xf = _tail(proj, yb, xf, p3d, i, vn_g, vn_b, w_s, bst, w_pa, w_pb, w_out, w_pg, w_pe,
                   ln_g, ln_b)
    return xf.reshape(batch, seq, d)
```

```python
import math

import jax
import jax.numpy as jnp
from jax import lax
from jax.experimental import pallas as pl
from jax.experimental.pallas import tpu as pltpu

F32 = jnp.float32
BF16 = jnp.bfloat16

D_MODEL = 1024
DEPTH = 4
PLE_DIM = 256
CHUNK = 128
A_GROUPS = 8
HEAD_DIM = 64
LANES = 128
IN_WIDTH = 9 * D_MODEL
ALPHA = (2 * DEPTH) ** 0.25
LN_EPS = 1e-5
LOG2E = math.log2(math.e)
Q_SCALE = HEAD_DIM ** -0.5 * LOG2E

COL_U, COL_V, COL_GA, COL_Q, COL_K, COL_VB, COL_GB, COL_MA, COL_MB = range(9)

PROJ_TM = 2048
PROJ_TN = 1536
ATT_T = 256
ATT_LG = 8
ATT_QT = 2
ATT_SKIP_LOG2 = -150.0
TAIL_TM = 512
TAIL_SUB = 256
V7X_VMEM_BYTES = 64 * 1024 * 1024
VMEM_LIMIT = V7X_VMEM_BYTES - 4 * 1024 * 1024
SOFTPLUS_CLAMP_LOG2 = 64.0


def _dot(a, b):
    return jnp.dot(a, b, preferred_element_type=F32)


def _dot_nt(a, b):
    return lax.dot_general(a, b, (((1,), (1,)), ((), ())), preferred_element_type=F32)


def _layer_spec(layer, shape):
    return pl.BlockSpec((None,) + shape, lambda *_: (layer,) + (0,) * len(shape))


def _in_proj_kernel(x_ref, w_ref, o_ref):
    acc = _dot(x_ref[...].astype(BF16), w_ref[...].astype(BF16))
    col = pl.program_id(1) * PROJ_TN + lax.broadcasted_iota(jnp.int32, (1, PROJ_TN), 1)
    is_q = jnp.logical_and(col >= COL_Q * D_MODEL, col < (COL_Q + 1) * D_MODEL)
    o_ref[...] = (acc * jnp.where(is_q, Q_SCALE, 1.0)).astype(BF16)


def _in_proj(x2d, w_in, layer):
    n = x2d.shape[0]
    return pl.pallas_call(
        _in_proj_kernel,
        out_shape=jax.ShapeDtypeStruct((n, IN_WIDTH), BF16),
        grid=(n // PROJ_TM, IN_WIDTH // PROJ_TN),
        in_specs=[
            pl.BlockSpec((PROJ_TM, D_MODEL), lambda i, j: (i, 0)),
            pl.BlockSpec((None, D_MODEL, PROJ_TN), lambda i, j: (layer, 0, j)),
        ],
        out_specs=pl.BlockSpec((PROJ_TM, PROJ_TN), lambda i, j: (i, j)),
        compiler_params=pltpu.CompilerParams(
            dimension_semantics=("arbitrary", "arbitrary"), vmem_limit_bytes=VMEM_LIMIT),
        name="in_proj",
    )(x2d, w_in)


def _attn_weights(q2, kb, carry, upper, mask):
    s = _dot_nt(q2, kb)
    sp = jnp.maximum(s, jnp.log(1.0 + jnp.exp2(jnp.minimum(s, SOFTPLUS_CLAMP_LOG2))) * LOG2E)
    spm = sp if mask is None else jnp.where(mask, sp, 0.0)
    later = _dot(spm.astype(BF16), upper)
    w = jnp.exp2((s - sp) + (carry - later))
    if mask is not None:
        w = jnp.where(mask, w, 0.0)
    return w.astype(BF16), carry - (later[:, 0:1] + spm[:, 0:1])


def _attn_kernel(q_ref, k_ref, v_ref, g_ref, o_ref, acc_ref, carry_ref):
    t = ATT_T
    step = pl.program_id(2)
    lane = lax.broadcasted_iota(jnp.int32, (t, LANES), 1)
    head0 = lane < HEAD_DIM
    row = lax.broadcasted_iota(jnp.int32, (t, t), 0)
    col = lax.broadcasted_iota(jnp.int32, (t, t), 1)
    upper = (row > col).astype(BF16)
    strict1 = col < row
    strict = jnp.concatenate([strict1, strict1], axis=0)

    def lanes(g):
        return slice(g * LANES, (g + 1) * LANES)

    def tile(ref, idx, g):
        return ref[pl.ds(pl.multiple_of(idx * t, t), t), lanes(g)]

    q2 = {}
    m = None
    for u in range(ATT_QT):
        qi = step * ATT_QT + u
        rows = slice(u * t, (u + 1) * t)
        prev = jnp.maximum(qi - 1, 0)
        has_prev = qi > 0
        for g in range(ATT_LG):
            q = q_ref[rows, lanes(g)]
            zero = jnp.zeros_like(q)
            q2[u, g] = jnp.concatenate(
                [jnp.where(head0, q, zero), jnp.where(head0, zero, q)], axis=0)
            w0, c0 = _attn_weights(q2[u, g], tile(k_ref, qi, g),
                                   jnp.zeros((2 * t, 1), F32), upper, strict)
            c0 = jnp.where(has_prev, c0, -jnp.inf)
            w1, c1 = _attn_weights(q2[u, g], tile(k_ref, prev, g), c0, upper, None)
            acc_ref[u, g] = _dot(
                jnp.concatenate([w0, w1], axis=1),
                jnp.concatenate([tile(v_ref, qi, g), tile(v_ref, prev, g)], axis=0))
            carry_ref[u, g] = c1
            cm = jnp.max(c1)
            m = cm if m is None else jnp.maximum(m, cm)

    def cond(state):
        j, worst = state
        return jnp.logical_and((step * ATT_QT + ATT_QT - 1) - 2 - j >= 0, worst > ATT_SKIP_LOG2)

    def body(state):
        j, _ = state
        worst = None
        for u in range(ATT_QT):
            idx = step * ATT_QT + u - 2 - j
            live = jnp.where(idx >= 0, 0.0, -jnp.inf)
            idx = jnp.maximum(idx, 0)
            for g in range(ATT_LG):
                w, c = _attn_weights(q2[u, g], tile(k_ref, idx, g),
                                     carry_ref[u, g] + live, upper, None)
                acc_ref[u, g] += _dot(w, tile(v_ref, idx, g))
                carry_ref[u, g] = c
                cm = jnp.max(c)
                worst = cm if worst is None else jnp.maximum(worst, cm)
        return j + 1, worst

    lax.while_loop(cond, body, (0, m))

    for u in range(ATT_QT):
        rows = slice(u * t, (u + 1) * t)
        for g in range(ATT_LG):
            o = jnp.where(head0, acc_ref[u, g, 0:t, :], acc_ref[u, g, t:2 * t, :])
            gate = g_ref[rows, lanes(g)].astype(F32)
            o_ref[rows, lanes(g)] = (o * (gate * jax.nn.sigmoid(gate))).astype(BF16)


def _attention(proj, batch, seq):
    n = proj.shape[0]
    t = ATT_T * ATT_QT
    qt = seq // t
    w = ATT_LG * LANES
    hg = D_MODEL // w
    cpb = D_MODEL // w
    return pl.pallas_call(
        _attn_kernel,
        out_shape=jax.ShapeDtypeStruct((n, D_MODEL), BF16),
        grid=(batch, hg, qt),
        in_specs=[
            pl.BlockSpec((t, w), lambda b, h, i: (b * qt + i, COL_Q * cpb + h)),
            pl.BlockSpec((seq, w), lambda b, h, i: (b, COL_K * cpb + h)),
            pl.BlockSpec((seq, w), lambda b, h, i: (b, COL_VB * cpb + h)),
            pl.BlockSpec((t, w), lambda b, h, i: (b * qt + i, COL_GB * cpb + h)),
        ],
        out_specs=pl.BlockSpec((t, w), lambda b, h, i: (b * qt + i, h)),
        scratch_shapes=[pltpu.VMEM((ATT_QT, ATT_LG, 2 * ATT_T, LANES), F32),
                        pltpu.VMEM((ATT_QT, ATT_LG, 2 * ATT_T, 1), F32)],
        compiler_params=pltpu.CompilerParams(
            dimension_semantics=("arbitrary", "arbitrary", "arbitrary"),
            vmem_limit_bytes=VMEM_LIMIT),
        name="attention",
    )(proj, proj, proj, proj)


def _layer_norm(x, g, b):
    mu = jnp.mean(x, axis=-1, keepdims=True)
    d = x - mu
    var = jnp.mean(d * d, axis=-1, keepdims=True)
    return d * lax.rsqrt(var + LN_EPS) * g + b


def _tail_kernel(u_ref, v_ref, ga_ref, ma_ref, mb_ref, yb_ref, x_ref, p_ref,
                 vng_ref, vnb_ref, ws_ref, bst_ref, wpa_ref, wpb_ref, wout_ref, wpg_ref,
                 wpe_ref, lng_ref, lnb_ref, o_ref, vn_ref, ya_ref):
    row = lax.broadcasted_iota(jnp.int32, (CHUNK, CHUNK), 0)
    col = lax.broadcasted_iota(jnp.int32, (CHUNK, CHUNK), 1)
    causal = col <= row
    for r0 in range(0, TAIL_TM, TAIL_SUB):
        rows = slice(r0, r0 + TAIL_SUB)
        yb_p = _dot(yb_ref[rows, :], wpb_ref[...])
        vn_ref[rows, :] = _layer_norm(
            v_ref[rows, :].astype(F32), vng_ref[...], vnb_ref[...]).astype(BF16)
        for g in range(A_GROUPS):
            cs = slice(g * LANES, (g + 1) * LANES)
            wsg = jnp.where(causal, ws_ref[g], jnp.zeros((CHUNK, CHUNK), BF16))
            bias = bst_ref[:, g:g + 1]
            for c0 in range(r0, r0 + TAIL_SUB, CHUNK):
                rs = slice(c0, c0 + CHUNK)
                mixed = _dot(wsg, vn_ref[rs, cs]) + bias
                ga = ga_ref[rs, cs].astype(F32)
                ya = u_ref[rs, cs].astype(F32) * mixed * (ga * jax.nn.sigmoid(ga))
                ya_ref[rs, cs] = ya.astype(BF16)
        ya_p = _dot(ya_ref[rows, :], wpa_ref[...])
        merged = (jax.nn.sigmoid(ma_ref[rows, :].astype(F32)) * ya_p
                  + jax.nn.sigmoid(mb_ref[rows, :].astype(F32)) * yb_p)
        y = _dot(merged.astype(BF16), wout_ref[...])
        h = ALPHA * x_ref[rows, :] + y
        gate = jax.nn.sigmoid(_dot(h.astype(BF16), wpg_ref[...]))
        pe = _dot(p_ref[rows, :].astype(BF16), wpe_ref[...])
        h = h + pe * gate
        o_ref[rows, :] = _layer_norm(h, lng_ref[...], lnb_ref[...])


def _tail(proj, yb, x2d, p3d, layer, vn_g, vn_b, ws, bst, w_pa, w_pb, w_out, w_pg, w_pe,
          ln_g, ln_b):
    n = x2d.shape[0]
    tm = TAIL_TM

    def seg(c):
        return pl.BlockSpec((tm, D_MODEL), lambda i, c=c: (i, c))

    def const(shape):
        return _layer_spec(layer, shape)

    return pl.pallas_call(
        _tail_kernel,
        out_shape=jax.ShapeDtypeStruct((n, D_MODEL), F32),
        grid=(n // tm,),
        in_specs=[
            seg(COL_U), seg(COL_V), seg(COL_GA), seg(COL_MA), seg(COL_MB),
            seg(0), seg(0),
            pl.BlockSpec((None, tm, PLE_DIM), lambda i: (layer, i, 0)),
            const((1, D_MODEL)), const((1, D_MODEL)),
            const((A_GROUPS, CHUNK, CHUNK)), const((CHUNK, A_GROUPS)),
            const((D_MODEL, D_MODEL)), const((D_MODEL, D_MODEL)),
            const((D_MODEL, D_MODEL)), const((D_MODEL, D_MODEL)),
            const((PLE_DIM, D_MODEL)),
            const((1, D_MODEL)), const((1, D_MODEL)),
        ],
        out_specs=pl.BlockSpec((tm, D_MODEL), lambda i: (i, 0)),
        scratch_shapes=[pltpu.VMEM((tm, D_MODEL), BF16), pltpu.VMEM((tm, D_MODEL), BF16)],
        compiler_params=pltpu.CompilerParams(
            dimension_semantics=("arbitrary",), vmem_limit_bytes=VMEM_LIMIT),
        name="tail",
    )(proj, proj, proj, proj, proj, yb, x2d, p3d, vn_g, vn_b, ws, bst,
      w_pa, w_pb, w_out, w_pg, w_pe, ln_g, ln_b)


def kernel(x, p, w_in, vn_g, vn_b, w_s, b_s, w_pa, w_pb, w_out, w_pe, w_pg, ln_g, ln_b):
    batch, seq, d = x.shape
    n = batch * seq
    assert d == D_MODEL and seq % (ATT_T * ATT_QT) == 0 and n % PROJ_TM == 0 and n % TAIL_TM == 0
    assert IN_WIDTH % PROJ_TN == 0
    depth = w_in.shape[0]
    assert depth == DEPTH
    xf = x.reshape(n, d)
    p3d = p.reshape(depth, n, PLE_DIM)
    row = lambda a: a.reshape(depth, 1, d)
    w_s, w_pa, w_pb, w_out, w_pg, w_pe = (
        a.astype(BF16) for a in (w_s, w_pa, w_pb, w_out, w_pg, w_pe))
    vn_g, vn_b, ln_g, ln_b = row(vn_g), row(vn_b), row(ln_g), row(ln_b)
    bst = jnp.swapaxes(b_s, 1, 2)
    for i in range(depth):
        proj = _in_proj(xf, w_in, i)
        yb = _attention(proj, batch, seq)
        xf = _tail(proj, yb, xf, p3d, i, vn_g, vn_b, w_s, bst, w_pa, w_pb, w_out, w_pg, w_pe,
                   ln_g, ln_b)
    return xf.reshape(batch, seq, d)
```

```python
import math

import jax
import jax.numpy as jnp
from jax import lax
from jax.experimental import pallas as pl
from jax.experimental.pallas import tpu as pltpu

F32 = jnp.float32
BF16 = jnp.bfloat16

D_MODEL = 1024
DEPTH = 4
PLE_DIM = 256
CHUNK = 128
A_GROUPS = 8
HEAD_DIM = 64
LANES = 128
IN_WIDTH = 9 * D_MODEL
ALPHA = (2 * DEPTH) ** 0.25
LN_EPS = 1e-5
LOG2E = math.log2(math.e)
Q_SCALE = HEAD_DIM ** -0.5 * LOG2E

COL_U, COL_V, COL_GA, COL_Q, COL_K, COL_VB, COL_GB, COL_MA, COL_MB = range(9)

PROJ_TM = 2048
PROJ_TN = 1536
ATT_T = 256
ATT_LG = 8
ATT_QT = 2
ATT_SKIP_LOG2 = -150.0
TAIL_TM = 512
TAIL_SUB = 256
V7X_VMEM_BYTES = 64 * 1024 * 1024
VMEM_LIMIT = V7X_VMEM_BYTES - 4 * 1024 * 1024
SOFTPLUS_CLAMP_LOG2 = 64.0


def _dot(a, b):
    return jnp.dot(a, b, preferred_element_type=F32)


def _dot_nt(a, b):
    return lax.dot_general(a, b, (((1,), (1,)), ((), ())), preferred_element_type=F32)


def _layer_spec(layer, shape):
    return pl.BlockSpec((None,) + shape, lambda *_: (layer,) + (0,) * len(shape))


def _in_proj_kernel(x_ref, w_ref, o_ref):
    acc = _dot(x_ref[...].astype(BF16), w_ref[...].astype(BF16))
    col = pl.program_id(1) * PROJ_TN + lax.broadcasted_iota(jnp.int32, (1, PROJ_TN), 1)
    is_q = jnp.logical_and(col >= COL_Q * D_MODEL, col < (COL_Q + 1) * D_MODEL)
    o_ref[...] = (acc * jnp.where(is_q, Q_SCALE, 1.0)).astype(BF16)


def _in_proj(x2d, w_in, layer):
    n = x2d.shape[0]
    return pl.pallas_call(
        _in_proj_kernel,
        out_shape=jax.ShapeDtypeStruct((n, IN_WIDTH), BF16),
        grid=(n // PROJ_TM, IN_WIDTH // PROJ_TN),
        in_specs=[
            pl.BlockSpec((PROJ_TM, D_MODEL), lambda i, j: (i, 0)),
            pl.BlockSpec((None, D_MODEL, PROJ_TN), lambda i, j: (layer, 0, j)),
        ],
        out_specs=pl.BlockSpec((PROJ_TM, PROJ_TN), lambda i, j: (i, j)),
        compiler_params=pltpu.CompilerParams(
            dimension_semantics=("arbitrary", "arbitrary"), vmem_limit_bytes=VMEM_LIMIT),
        name="in_proj",
    )(x2d, w_in)


def _softplus2(s):
    return jnp.maximum(s, jnp.log(1.0 + jnp.exp2(jnp.minimum(s, SOFTPLUS_CLAMP_LOG2))) * LOG2E)


def _attn_weights(q2, kb, carry, upper):
    s = _dot_nt(q2, kb)
    sp = _softplus2(s)
    later = _dot(sp.astype(BF16), upper)
    w = jnp.exp2((s - sp) + (carry - later))
    return w.astype(BF16), carry - (later[:, 0:1] + sp[:, 0:1])


def _attn_first_two(q2, k0, k1, upper, mask, has_prev):
    t = k0.shape[0]
    s0, s1 = _dot_nt(q2, k0), _dot_nt(q2, k1)
    sp0, sp1 = _softplus2(s0), _softplus2(s1)
    spm0 = jnp.where(mask, sp0, 0.0)
    later = _dot(jnp.concatenate([spm0, sp1], axis=0).astype(BF16), upper)
    later0, later1 = later[0:2 * t], later[2 * t:4 * t]
    w0 = jnp.where(mask, jnp.exp2((s0 - sp0) - later0), 0.0)
    c0 = jnp.where(has_prev, -(later0[:, 0:1] + spm0[:, 0:1]), -jnp.inf)
    w1 = jnp.exp2((s1 - sp1) + (c0 - later1))
    return w0.astype(BF16), w1.astype(BF16), c0 - (later1[:, 0:1] + sp1[:, 0:1])


def _attn_kernel(q_ref, k_ref, v_ref, g_ref, o_ref, acc_ref, carry_ref):
    t = ATT_T
    step = pl.program_id(2)
    lane = lax.broadcasted_iota(jnp.int32, (t, LANES), 1)
    head0 = lane < HEAD_DIM
    row = lax.broadcasted_iota(jnp.int32, (t, t), 0)
    col = lax.broadcasted_iota(jnp.int32, (t, t), 1)
    upper = (row > col).astype(BF16)
    strict1 = col < row
    strict = jnp.concatenate([strict1, strict1], axis=0)

    def lanes(g):
        return slice(g * LANES, (g + 1) * LANES)

    def tile(ref, idx, g):
        return ref[pl.ds(pl.multiple_of(idx * t, t), t), lanes(g)]

    q2 = {}
    m = None
    for u in range(ATT_QT):
        qi = step * ATT_QT + u
        rows = slice(u * t, (u + 1) * t)
        prev = jnp.maximum(qi - 1, 0)
        has_prev = qi > 0
        for g in range(ATT_LG):
            q = q_ref[rows, lanes(g)]
            zero = jnp.zeros_like(q)
            q2[u, g] = jnp.concatenate(
                [jnp.where(head0, q, zero), jnp.where(head0, zero, q)], axis=0)
            w0, w1, c1 = _attn_first_two(q2[u, g], tile(k_ref, qi, g), tile(k_ref, prev, g),
                                         upper, strict, has_prev)
            acc_ref[u, g] = _dot(
                jnp.concatenate([w0, w1], axis=1),
                jnp.concatenate([tile(v_ref, qi, g), tile(v_ref, prev, g)], axis=0))
            carry_ref[u, g] = c1
            cm = jnp.max(c1)
            m = cm if m is None else jnp.maximum(m, cm)

    def cond(state):
        j, worst = state
        return jnp.logical_and((step * ATT_QT + ATT_QT - 1) - 2 - j >= 0, worst > ATT_SKIP_LOG2)

    def body(state):
        j, _ = state
        worst = None
        for u in range(ATT_QT):
            idx = step * ATT_QT + u - 2 - j
            live = jnp.where(idx >= 0, 0.0, -jnp.inf)
            idx = jnp.maximum(idx, 0)
            for g in range(ATT_LG):
                w, c = _attn_weights(q2[u, g], tile(k_ref, idx, g),
                                     carry_ref[u, g] + live, upper)
                acc_ref[u, g] += _dot(w, tile(v_ref, idx, g))
                carry_ref[u, g] = c
                cm = jnp.max(c)
                worst = cm if worst is None else jnp.maximum(worst, cm)
        return j + 1, worst

    lax.while_loop(cond, body, (0, m))

    for u in range(ATT_QT):
        rows = slice(u * t, (u + 1) * t)
        for g in range(ATT_LG):
            o = jnp.where(head0, acc_ref[u, g, 0:t, :], acc_ref[u, g, t:2 * t, :])
            gate = g_ref[rows, lanes(g)].astype(F32)
            o_ref[rows, lanes(g)] = (o * (gate * jax.nn.sigmoid(gate))).astype(BF16)


def _attention(proj, batch, seq):
    n = proj.shape[0]
    t = ATT_T * ATT_QT
    qt = seq // t
    w = ATT_LG * LANES
    hg = D_MODEL // w
    cpb = D_MODEL // w
    return pl.pallas_call(
        _attn_kernel,
        out_shape=jax.ShapeDtypeStruct((n, D_MODEL), BF16),
        grid=(batch, hg, qt),
        in_specs=[
            pl.BlockSpec((t, w), lambda b, h, i: (b * qt + i, COL_Q * cpb + h)),
            pl.BlockSpec((seq, w), lambda b, h, i: (b, COL_K * cpb + h)),
            pl.BlockSpec((seq, w), lambda b, h, i: (b, COL_VB * cpb + h)),
            pl.BlockSpec((t, w), lambda b, h, i: (b * qt + i, COL_GB * cpb + h)),
        ],
        out_specs=pl.BlockSpec((t, w), lambda b, h, i: (b * qt + i, h)),
        scratch_shapes=[pltpu.VMEM((ATT_QT, ATT_LG, 2 * ATT_T, LANES), F32),
                        pltpu.VMEM((ATT_QT, ATT_LG, 2 * ATT_T, 1), F32)],
        compiler_params=pltpu.CompilerParams(
            dimension_semantics=("arbitrary", "arbitrary", "arbitrary"),
            vmem_limit_bytes=VMEM_LIMIT),
        name="attention",
    )(proj, proj, proj, proj)


def _layer_norm(x, g, b):
    mu = jnp.mean(x, axis=-1, keepdims=True)
    d = x - mu
    var = jnp.mean(d * d, axis=-1, keepdims=True)
    return d * lax.rsqrt(var + LN_EPS) * g + b


def _tail_kernel(u_ref, v_ref, ga_ref, ma_ref, mb_ref, yb_ref, x_ref, p_ref,
                 vng_ref, vnb_ref, ws_ref, bst_ref, wpa_ref, wpb_ref, wout_ref, wpg_ref,
                 wpe_ref, lng_ref, lnb_ref, o_ref, vn_ref, ya_ref):
    row = lax.broadcasted_iota(jnp.int32, (CHUNK, CHUNK), 0)
    col = lax.broadcasted_iota(jnp.int32, (CHUNK, CHUNK), 1)
    causal = col <= row
    for r0 in range(0, TAIL_TM, TAIL_SUB):
        rows = slice(r0, r0 + TAIL_SUB)
        yb_p = _dot(yb_ref[rows, :], wpb_ref[...])
        vn_ref[rows, :] = _layer_norm(
            v_ref[rows, :].astype(F32), vng_ref[...], vnb_ref[...]).astype(BF16)
        for g in range(A_GROUPS):
            cs = slice(g * LANES, (g + 1) * LANES)
            wsg = jnp.where(causal, ws_ref[g], jnp.zeros((CHUNK, CHUNK), BF16))
            bias = bst_ref[:, g:g + 1]
            for c0 in range(r0, r0 + TAIL_SUB, CHUNK):
                rs = slice(c0, c0 + CHUNK)
                mixed = _dot(wsg, vn_ref[rs, cs]) + bias
                ga = ga_ref[rs, cs].astype(F32)
                ya = u_ref[rs, cs].astype(F32) * mixed * (ga * jax.nn.sigmoid(ga))
                ya_ref[rs, cs] = ya.astype(BF16)
        ya_p = _dot(ya_ref[rows, :], wpa_ref[...])
        merged = (jax.nn.sigmoid(ma_ref[rows, :].astype(F32)) * ya_p
                  + jax.nn.sigmoid(mb_ref[rows, :].astype(F32)) * yb_p)
        y = _dot(merged.astype(BF16), wout_ref[...])
        h = ALPHA * x_ref[rows, :] + y
        gate = jax.nn.sigmoid(_dot(h.astype(BF16), wpg_ref[...]))
        pe = _dot(p_ref[rows, :].astype(BF16), wpe_ref[...])
        h = h + pe * gate
        o_ref[rows, :] = _layer_norm(h, lng_ref[...], lnb_ref[...])


def _tail(proj, yb, x2d, p3d, layer, vn_g, vn_b, ws, bst, w_pa, w_pb, w_out, w_pg, w_pe,
          ln_g, ln_b):
    n = x2d.shape[0]
    tm = TAIL_TM

    def seg(c):
        return pl.BlockSpec((tm, D_MODEL), lambda i, c=c: (i, c))

    def const(shape):
        return _layer_spec(layer, shape)

    return pl.pallas_call(
        _tail_kernel,
        out_shape=jax.ShapeDtypeStruct((n, D_MODEL), F32),
        grid=(n // tm,),
        in_specs=[
            seg(COL_U), seg(COL_V), seg(COL_GA), seg(COL_MA), seg(COL_MB),
            seg(0), seg(0),
            pl.BlockSpec((None, tm, PLE_DIM), lambda i: (layer, i, 0)),
            const((1, D_MODEL)), const((1, D_MODEL)),
            const((A_GROUPS, CHUNK, CHUNK)), const((CHUNK, A_GROUPS)),
            const((D_MODEL, D_MODEL)), const((D_MODEL, D_MODEL)),
            const((D_MODEL, D_MODEL)), const((D_MODEL, D_MODEL)),
            const((PLE_DIM, D_MODEL)),
            const((1, D_MODEL)), const((1, D_MODEL)),
        ],
        out_specs=pl.BlockSpec((tm, D_MODEL), lambda i: (i, 0)),
        scratch_shapes=[pltpu.VMEM((tm, D_MODEL), BF16), pltpu.VMEM((tm, D_MODEL), BF16)],
        compiler_params=pltpu.CompilerParams(
            dimension_semantics=("arbitrary",), vmem_limit_bytes=VMEM_LIMIT),
        name="tail",
    )(proj, proj, proj, proj, proj, yb, x2d, p3d, vn_g, vn_b, ws, bst,
      w_pa, w_pb, w_out, w_pg, w_pe, ln_g, ln_b)


def kernel(x, p, w_in, vn_g, vn_b, w_s, b_s, w_pa, w_pb, w_out, w_pe, w_pg, ln_g, ln_b):
    batch, seq, d = x.shape
    n = batch * seq
    assert d == D_MODEL and seq % (ATT_T * ATT_QT) == 0 and n % PROJ_TM == 0 and n % TAIL_TM == 0
    assert IN_WIDTH % PROJ_TN == 0
    depth = w_in.shape[0]
    assert depth == DEPTH
    xf = x.reshape(n, d)
    p3d = p.reshape(depth, n, PLE_DIM)
    row = lambda a: a.reshape(depth, 1, d)
    w_s, w_pa, w_pb, w_out, w_pg, w_pe = (
        a.astype(BF16) for a in (w_s, w_pa, w_pb, w_out, w_pg, w_pe))
    vn_g, vn_b, ln_g, ln_b = row(vn_g), row(vn_b), row(ln_g), row(ln_b)
    bst = jnp.swapaxes(b_s, 1, 2)
    for i in range(depth):
        proj = _in_proj(xf, w_in, i)
        yb = _attention(proj, batch, seq)
        xf = _tail(proj, yb, xf, p3d, i, vn_g, vn_b, w_s, bst, w_pa, w_pb, w_out, w_pg, w_pe,
                   ln_g, ln_b)
    return xf.reshape(batch, seq, d)
```
